```python
import math
import jax
import jax.numpy as jnp
from jax import lax
import numpy as np


D_MODEL = 2048
BATCH = 8
SEQ = 4096
DEPTH = 4
DEC_BATCH = 1
DEC_SEQ = 16384
PAST_LEN = 128

N_META = 16
D_RNN = 1024
RNN_BLOCKS = 8
RNN_BLOCK_DIM = D_RNN // RNN_BLOCKS
CONV_A = 4
RG_C = 8.0
D_HY = 1024
CONV_B = 3
HY_BANDS = 16
HY_EMB = 2 * HY_BANDS + 1
HY_FILT = 64
HY_FAST_DECAY = 0.3
HY_SLOW_DECAY = 1.5
HY_TARGET = 1e-2
N_HEADS = 16
Q_LORA = 512
KV_LORA = 512
QK_NOPE = 128
QK_ROPE = 64
V_DIM = 128
ROPE_THETA = 10000.0
Q_BLOCK = 128
N_BRANCH = 3
COLS_A = 2 * D_RNN
COLS_B = 3 * D_HY
COLS_C = Q_LORA + KV_LORA + QK_ROPE
COLS_G = N_BRANCH * D_MODEL
D_IN = COLS_A + COLS_B + COLS_C + COLS_G
N_GROUPS = 8
EXPERTS_PER_GROUP = 8
N_EXPERTS = N_GROUPS * EXPERTS_PER_GROUP
TOP_K = 2
D_EXPERT = 512
MOE_BLOCK = 128
DN_ALPHA = (2 * DEPTH) ** 0.25
DN_BETA = (8 * DEPTH) ** -0.25
LN_EPS = 1e-5
RMS_EPS = 1e-6

kernel_name = 'hybrid_rglru_hyena_mla_hmoe_encoder'

F32 = jnp.float32


def layer_norm(x, g, b):
    xf = x.astype(F32)
    mu = jnp.mean(xf, axis=-1, keepdims=True)
    xc = xf - mu
    var = jnp.mean(xc * xc, axis=-1, keepdims=True)
    return (xc * lax.rsqrt(var + LN_EPS) * g.astype(F32) + b.astype(F32)).astype(x.dtype)


def rms_norm(x, g):
    xf = x.astype(F32)
    y = xf * lax.rsqrt(jnp.mean(xf * xf, axis=-1, keepdims=True) + RMS_EPS)
    return (y * g.astype(F32)).astype(x.dtype)


def depthwise_conv(x, w, b, pad):
    y = lax.conv_general_dilated(x, w[:, None, :].astype(x.dtype), window_strides=(1,), padding=[pad], dimension_numbers=('NWC', 'WIO', 'NWC'), feature_group_count=x.shape[-1])
    return y + b.astype(x.dtype)


def rope_tables(L):
    inv = 1.0 / (ROPE_THETA ** (jnp.arange(0, QK_ROPE, 2, dtype=F32) / QK_ROPE))
    ang = jnp.arange(L, dtype=F32)[:, None] * inv[None, :]
    return jnp.cos(ang), jnp.sin(ang)


def apply_rope(x, cos, sin):
    xf = x.astype(F32)
    half = QK_ROPE // 2
    x1, x2 = xf[..., :half], xf[..., half:]
    return jnp.concatenate([x1 * cos - x2 * sin, x1 * sin + x2 * cos], axis=-1).astype(x.dtype)


def _linear_recurrence_combine(c1, c2):
    a1, b1 = c1
    a2, b2 = c2
    return a1 * a2, a2 * b1 + b2


def rglru_mixer(pa, conv_w, conv_b, wa, ba, wx, bx, lam):
    B_, L_, _ = pa.shape
    xb = depthwise_conv(pa[..., :D_RNN], conv_w, conv_b, (1, 2))
    gb = pa[..., D_RNN:]
    xr = xb.reshape(B_, L_, RNN_BLOCKS, RNN_BLOCK_DIM)
    xf = xb.astype(F32)
    hs = []
    for d in range(2):
        r = jax.nn.sigmoid((jnp.einsum('blhi,hij->blhj', xr, wa[d]).reshape(B_, L_, D_RNN) + ba[d]).astype(F32))
        i = jax.nn.sigmoid((jnp.einsum('blhi,hij->blhj', xr, wx[d]).reshape(B_, L_, D_RNN) + bx[d]).astype(F32))
        log_a = -RG_C * r * jax.nn.softplus(-lam[d].astype(F32))
        a = jnp.exp(log_a)
        u = jnp.sqrt(-jnp.expm1(2.0 * log_a)) * (i * xf)
        _, h = lax.associative_scan(_linear_recurrence_combine, (a, u), reverse=(d == 1), axis=1)
        hs.append(h)
    return (jax.nn.gelu(gb.astype(F32)) * (hs[0] + hs[1])).astype(pa.dtype)


def hyena_filter(L, w1, b1, w2, b2, w3, freq):
    t01 = jnp.linspace(0.0, 1.0, L, dtype=F32)[:, None]
    ang = (2.0 * math.pi / L) * jnp.arange(L, dtype=F32)[:, None]
    bands = jnp.linspace(1e-4, HY_BANDS - 1, HY_BANDS, dtype=F32)[None, :]
    emb = jnp.concatenate([t01, jnp.cos(bands * ang), -jnp.sin(bands * ang)], axis=-1)
    fr = freq.astype(F32)
    z = jnp.sin(fr * (emb @ w1.astype(F32) + b1.astype(F32)))
    z = jnp.sin(fr * (z @ w2.astype(F32) + b2.astype(F32)))
    z = z @ w3.astype(F32)
    rates = jnp.abs(jnp.linspace(math.log(HY_TARGET) / HY_SLOW_DECAY, math.log(HY_TARGET) / HY_FAST_DECAY, D_HY, dtype=F32))
    decay = jnp.exp(-t01 * rates[None, :])
    fwd = z[:, :D_HY] * decay
    bwd = z[1:, D_HY:] * decay[1:]
    circ = jnp.concatenate([fwd, jnp.zeros((1, D_HY), F32), bwd[::-1]], axis=0)
    return circ / jnp.sum(jnp.abs(circ), axis=0, keepdims=True)


def hyena_mixer(pb, conv_w, conv_b, filt, skip):
    L_ = pb.shape[1]
    zc = depthwise_conv(pb, conv_w, conv_b, (1, 1))
    x0, x1, v = zc[..., :D_HY], zc[..., D_HY:2 * D_HY], zc[..., 2 * D_HY:]
    s = (x1 * v).astype(F32)
    n = 2 * L_
    y = jnp.fft.irfft(jnp.fft.rfft(s, n=n, axis=1) * jnp.fft.rfft(filt, n=n, axis=0)[None], n=n, axis=1)[:, :L_]
    y = y + s * skip.astype(F32)
    return (x0.astype(F32) * y).astype(pb.dtype)


def mla_mixer(pc, q_norm_g, w_uq, kv_norm_g, w_ukv, cos, sin):
    B_, L_, _ = pc.shape
    cq = pc[..., :Q_LORA]
    ckv = pc[..., Q_LORA:Q_LORA + KV_LORA]
    kr = pc[..., Q_LORA + KV_LORA:]
    q = (rms_norm(cq, q_norm_g) @ w_uq).reshape(B_, L_, N_HEADS, QK_NOPE + QK_ROPE)
    q_nope = q[..., :QK_NOPE]
    q_rope = apply_rope(q[..., QK_NOPE:], cos[:, None, :], sin[:, None, :])
    kv = (rms_norm(ckv, kv_norm_g) @ w_ukv).reshape(B_, L_, N_HEADS, QK_NOPE + V_DIM)
    k_nope, v = kv[..., :QK_NOPE], kv[..., QK_NOPE:]
    k_rope = apply_rope(kr, cos, sin)
    scale = (QK_NOPE + QK_ROPE) ** -0.5

    def attend(qn, qr):
        s = jnp.einsum('bqhd,bkhd->bhqk', qn, k_nope) + jnp.einsum('bqhr,bkr->bhqk', qr, k_rope)
        p = jax.nn.softmax(s.astype(F32) * scale, axis=-1).astype(v.dtype)
        return jnp.einsum('bhqk,bkhd->bqhd', p, v)

    o_meta = attend(q_nope[:, :N_META], q_rope[:, :N_META])
    S_ = L_ - N_META
    nb = S_ // Q_BLOCK

    def to_blocks(t):
        return t[:, N_META:].reshape(B_, nb, Q_BLOCK, N_HEADS, t.shape[-1]).transpose(1, 0, 2, 3, 4)

    o_main = lax.map(lambda qs: attend(qs[0], qs[1]), (to_blocks(q_nope), to_blocks(q_rope)))
    o_main = o_main.transpose(1, 0, 2, 3, 4).reshape(B_, S_, N_HEADS, V_DIM)
    return jnp.concatenate([o_meta, o_main], axis=1).reshape(B_, L_, N_HEADS * V_DIM)


def routed_moe(u, w_rg, b_rg, w_re, b_re, w_g, w_u, w_d):
    B_, L_, _ = u.shape
    T = B_ * L_
    t = u.reshape(T, D_MODEL)
    grp_logits = (t @ w_rg + b_rg).astype(F32)
    grp_prob = jax.nn.softmax(grp_logits, axis=-1)
    g_sel = jnp.argmax(grp_logits, axis=-1).astype(jnp.int32)
    p_sel = jnp.take_along_axis(grp_prob, g_sel[:, None], axis=-1)
    exp_logits = (t @ w_re + b_re).astype(F32).reshape(T, N_GROUPS, EXPERTS_PER_GROUP)
    in_grp = jnp.take_along_axis(exp_logits, g_sel[:, None, None], axis=1)[:, 0]
    top_v, top_i = lax.top_k(in_grp, TOP_K)
    gate = jax.nn.softmax(top_v, axis=-1) * p_sel
    eid = (g_sel[:, None] * EXPERTS_PER_GROUP + top_i.astype(jnp.int32)).reshape(-1)
    wgt = gate.reshape(-1)
    A = T * TOP_K
    tok = jnp.repeat(jnp.arange(T, dtype=jnp.int32), TOP_K)
    order = jnp.argsort(eid)
    e_sorted = eid[order]
    counts = jnp.bincount(eid, length=N_EXPERTS).astype(jnp.int32)
    padded = (counts + MOE_BLOCK - 1) // MOE_BLOCK * MOE_BLOCK
    start = jnp.cumsum(counts) - counts
    pend = jnp.cumsum(padded)
    pstart = pend - padded
    dest = pstart[e_sorted] + jnp.arange(A, dtype=jnp.int32) - start[e_sorted]
    n_blocks = (A + N_EXPERTS * (MOE_BLOCK - 1) + MOE_BLOCK - 1) // MOE_BLOCK
    P = n_blocks * MOE_BLOCK
    row_tok = jnp.full((P,), T, jnp.int32).at[dest].set(tok[order])
    row_w = jnp.zeros((P,), F32).at[dest].set(wgt[order])
    blk_e = jnp.minimum(jnp.searchsorted(pend, jnp.arange(n_blocks, dtype=jnp.int32) * MOE_BLOCK, side='right'), N_EXPERTS - 1).astype(jnp.int32)
    t_pad = jnp.concatenate([t, jnp.zeros((1, D_MODEL), t.dtype)], axis=0)
    xs = t_pad[row_tok].reshape(n_blocks, MOE_BLOCK, D_MODEL)

    def expert_block(args):
        xb, e = args
        return (jax.nn.silu(xb @ w_g[e]) * (xb @ w_u[e])) @ w_d[e]

    ys = lax.map(expert_block, (xs, blk_e)).reshape(P, D_MODEL)
    out = jnp.zeros((T + 1, D_MODEL), t.dtype).at[row_tok].add(ys * row_w[:, None].astype(ys.dtype))[:T]
    return out.reshape(B_, L_, D_MODEL)


def encoder_layer(h, l, W, cos, sin):
    L_ = h.shape[1]
    proj = h @ W['w_in'][l]
    e_a = COLS_A
    e_b = e_a + COLS_B
    e_c = e_b + COLS_C
    pa, pb, pc, pg = proj[..., :e_a], proj[..., e_a:e_b], proj[..., e_b:e_c], proj[..., e_c:]
    ya = rglru_mixer(pa, W['conv_a_w'][l], W['conv_a_b'][l], W['rg_wa'][l], W['rg_ba'][l], W['rg_wx'][l], W['rg_bx'][l], W['rg_lambda'][l]) @ W['w_pa'][l]
    filt = hyena_filter(L_, W['hy_w1'][l], W['hy_b1'][l], W['hy_w2'][l], W['hy_b2'][l], W['hy_w3'][l], W['hy_freq'][l])
    yb = hyena_mixer(pb, W['conv_b_w'][l], W['conv_b_b'][l], filt, W['hy_skip'][l]) @ W['w_pb'][l]
    yc = mla_mixer(pc, W['q_norm_g'][l], W['w_uq'][l], W['kv_norm_g'][l], W['w_ukv'][l], cos, sin) @ W['w_pc'][l]
    g = jax.nn.sigmoid(pg.astype(F32)).astype(h.dtype)
    g_a, g_b, g_c = g[..., :D_MODEL], g[..., D_MODEL:2 * D_MODEL], g[..., 2 * D_MODEL:]
    mixed = (g_a * ya + g_b * yb + g_c * yc) @ W['w_out'][l]
    h = layer_norm(DN_ALPHA * h + mixed, W['ln1_g'][l], W['ln1_b'][l])
    moe = routed_moe(h, W['w_rg'][l], W['b_rg'][l], W['w_re'][l], W['b_re'][l], W['w_e_gate'][l], W['w_e_up'][l], W['w_e_down'][l])
    return layer_norm(DN_ALPHA * h + moe, W['ln2_g'][l], W['ln2_b'][l])


def encoder_trunk(x, W):
    B_, S_, _ = x.shape
    L_ = S_ + N_META
    meta = jnp.broadcast_to(W['meta'][None].astype(x.dtype), (B_, N_META, D_MODEL))
    h = layer_norm(jnp.concatenate([meta, x], axis=1), W['ln_emb_g'], W['ln_emb_b'])
    cos, sin = rope_tables(L_)
    for l in range(DEPTH):
        h = encoder_layer(h, l, W, cos, sin)
    return h[:, N_META:]


def setup_inputs(seed: int = 0) -> dict:
    key = jax.random.key(seed)
    keys = list(jax.random.split(key, 64))

    def nrm(shape, scale):
        return scale * jax.random.normal(keys.pop(), shape, F32)

    def gain(shape):
        return 1.0 + 0.02 * jax.random.normal(keys.pop(), shape, F32)

    def lru_lambda(shape):
        a0 = jax.random.uniform(keys.pop(), shape, F32, minval=0.9, maxval=0.999)
        return jnp.log(a0) - jnp.log1p(-a0)

    D = D_MODEL
    return {
        'x_prompt': nrm((BATCH, SEQ, D), 1.0),
        'x_sample': nrm((DEC_BATCH, DEC_SEQ, D), 1.0),
        'meta': nrm((N_META, D), 1.0),
        'ln_emb_g': gain((D,)),
        'ln_emb_b': nrm((D,), 0.02),
        'w_in': nrm((DEPTH, D, D_IN), D ** -0.5),
        'conv_a_w': nrm((DEPTH, CONV_A, D_RNN), CONV_A ** -0.5),
        'conv_a_b': nrm((DEPTH, D_RNN), 0.02),
        'rg_wa': nrm((DEPTH, 2, RNN_BLOCKS, RNN_BLOCK_DIM, RNN_BLOCK_DIM), RNN_BLOCK_DIM ** -0.5),
        'rg_ba': nrm((DEPTH, 2, D_RNN), 0.02),
        'rg_wx': nrm((DEPTH, 2, RNN_BLOCKS, RNN_BLOCK_DIM, RNN_BLOCK_DIM), RNN_BLOCK_DIM ** -0.5),
        'rg_bx': nrm((DEPTH, 2, D_RNN), 0.02),
        'rg_lambda': lru_lambda((DEPTH, 2, D_RNN)),
        'conv_b_w': nrm((DEPTH, CONV_B, 3 * D_HY), CONV_B ** -0.5),
        'conv_b_b': nrm((DEPTH, 3 * D_HY), 0.02),
        'hy_w1': nrm((DEPTH, HY_EMB, HY_FILT), HY_EMB ** -0.5),
        'hy_b1': nrm((DEPTH, HY_FILT), 0.1),
        'hy_w2': nrm((DEPTH, HY_FILT, HY_FILT), HY_FILT ** -0.5),
        'hy_b2': nrm((DEPTH, HY_FILT), 0.1),
        'hy_w3': nrm((DEPTH, HY_FILT, 2 * D_HY), HY_FILT ** -0.5),
        'hy_freq': 1.0 + nrm((DEPTH, HY_FILT), 0.1),
        'hy_skip': nrm((DEPTH, D_HY), 1.0),
        'q_norm_g': gain((DEPTH, Q_LORA)),
        'w_uq': nrm((DEPTH, Q_LORA, N_HEADS * (QK_NOPE + QK_ROPE)), Q_LORA ** -0.5),
        'kv_norm_g': gain((DEPTH, KV_LORA)),
        'w_ukv': nrm((DEPTH, KV_LORA, N_HEADS * (QK_NOPE + V_DIM)), KV_LORA ** -0.5),
        'w_pa': nrm((DEPTH, D_RNN, D), DN_BETA * D_RNN ** -0.5),
        'w_pb': nrm((DEPTH, D_HY, D), DN_BETA * D_HY ** -0.5),
        'w_pc': nrm((DEPTH, N_HEADS * V_DIM, D), DN_BETA * (N_HEADS * V_DIM) ** -0.5),
        'w_out': nrm((DEPTH, D, D), DN_BETA * D ** -0.5),
        'ln1_g': gain((DEPTH, D)),
        'ln1_b': nrm((DEPTH, D), 0.02),
        'w_rg': nrm((DEPTH, D, N_GROUPS), D ** -0.5),
        'b_rg': nrm((DEPTH, N_GROUPS), 0.01),
        'w_re': nrm((DEPTH, D, N_EXPERTS), D ** -0.5),
        'b_re': nrm((DEPTH, N_EXPERTS), 0.01),
        'w_e_gate': nrm((DEPTH, N_EXPERTS, D, D_EXPERT), D ** -0.5),
        'w_e_up': nrm((DEPTH, N_EXPERTS, D, D_EXPERT), D ** -0.5),
        'w_e_down': nrm((DEPTH, N_EXPERTS, D_EXPERT, D), DN_BETA * D_EXPERT ** -0.5),
        'ln2_g': gain((DEPTH, D)),
        'ln2_b': nrm((DEPTH, D), 0.02),
    }


def reference(x_prompt, x_sample, meta, ln_emb_g, ln_emb_b, w_in, conv_a_w, conv_a_b, rg_wa, rg_ba, rg_wx, rg_bx, rg_lambda, conv_b_w, conv_b_b, hy_w1, hy_b1, hy_w2, hy_b2, hy_w3, hy_freq, hy_skip, q_norm_g, w_uq, kv_norm_g, w_ukv, w_pa, w_pb, w_pc, w_out, ln1_g, ln1_b, w_rg, b_rg, w_re, b_re, w_e_gate, w_e_up, w_e_down, ln2_g, ln2_b):
    W = dict(meta=meta, ln_emb_g=ln_emb_g, ln_emb_b=ln_emb_b, w_in=w_in, conv_a_w=conv_a_w, conv_a_b=conv_a_b, rg_wa=rg_wa, rg_ba=rg_ba, rg_wx=rg_wx, rg_bx=rg_bx, rg_lambda=rg_lambda, conv_b_w=conv_b_w, conv_b_b=conv_b_b, hy_w1=hy_w1, hy_b1=hy_b1, hy_w2=hy_w2, hy_b2=hy_b2, hy_w3=hy_w3, hy_freq=hy_freq, hy_skip=hy_skip, q_norm_g=q_norm_g, w_uq=w_uq, kv_norm_g=kv_norm_g, w_ukv=w_ukv, w_pa=w_pa, w_pb=w_pb, w_pc=w_pc, w_out=w_out, ln1_g=ln1_g, ln1_b=ln1_b, w_rg=w_rg, b_rg=b_rg, w_re=w_re, b_re=b_re, w_e_gate=w_e_gate, w_e_up=w_e_up, w_e_down=w_e_down, ln2_g=ln2_g, ln2_b=ln2_b)
    y_prompt = encoder_trunk(x_prompt, W)
    y_sample = encoder_trunk(x_sample, W)
    return (y_prompt, y_sample)
```

```python
import functools
import math

import numpy as np
import jax
import jax.numpy as jnp
from jax import lax
from jax.experimental import pallas as pl
from jax.experimental.pallas import tpu as pltpu

F32 = jnp.float32
BF16 = jnp.bfloat16
I32 = jnp.int32

N_META = 16
RG_C = 8.0
HY_BANDS = 16
HY_FAST_DECAY = 0.3
HY_SLOW_DECAY = 1.5
HY_TARGET = 1e-2
N_HEADS = 16
QK_NOPE = 128
QK_ROPE = 64
V_DIM = 128
ROPE_THETA = 10000.0
N_GROUPS = 8
EXPERTS_PER_GROUP = 8
LN_EPS = 1e-5
RMS_EPS = 1e-6

V7X_VMEM_BYTES = 64 * 1024 * 1024
VMEM_LIMIT = 56 * 1024 * 1024
LANES = 128
HEAD_PAD = 2 * LANES

ROW_BLOCK = 512
MM_ROW_BLOCK = 1024
HALO = 16
HY_MAX_BLOCK = 4608
MOE_BLOCK = 256
ATT_TQ = 256
ATT_CK = 512


def _params(sem, vmem=VMEM_LIMIT):
    return pltpu.CompilerParams(dimension_semantics=sem, vmem_limit_bytes=vmem)


def _round_up(x, m):
    return (x + m - 1) // m * m


def _pick_tile(n, cands):
    for c in cands:
        if n % c == 0:
            return c
    return n


def _const_spec(shape):
    nd = len(shape)
    return pl.BlockSpec(shape, lambda *a: (0,) * nd, pipeline_mode=pl.Buffered(1))


def _layer_norm_val(x, g, b):
    mu = jnp.mean(x, axis=-1, keepdims=True)
    xc = x - mu
    var = jnp.mean(xc * xc, axis=-1, keepdims=True)
    return xc * lax.rsqrt(var + LN_EPS) * g + b


def _ln_kernel(x_ref, g_ref, b_ref, o_ref, ob_ref):
    y = _layer_norm_val(x_ref[...], g_ref[...], b_ref[...])
    o_ref[...] = y
    ob_ref[...] = y.astype(BF16)


def _layer_norm_rows(x, g, b):
    T, D = x.shape
    rb = ROW_BLOCK
    row = pl.BlockSpec((rb, D), lambda i: (i, 0))
    vec = pl.BlockSpec((1, D), lambda i: (0, 0))
    return pl.pallas_call(
        _ln_kernel,
        grid=(pl.cdiv(T, rb),),
        in_specs=[row, vec, vec],
        out_specs=[row, row],
        out_shape=[jax.ShapeDtypeStruct((T, D), F32), jax.ShapeDtypeStruct((T, D), BF16)],
        compiler_params=_params(("parallel",)),
        name="embed_ln",
    )(x, g.reshape(1, D), b.reshape(1, D))


def _mm_kernel(x_ref, w_ref, o_ref):
    o_ref[...] = jnp.dot(x_ref[...], w_ref[...], preferred_element_type=F32).astype(o_ref.dtype)


def _matmul(x, w, out_dtype, name):
    T, K = x.shape
    N = w.shape[1]
    rb = MM_ROW_BLOCK
    tn = _pick_tile(N, (512, 384, 256, 128))
    return pl.pallas_call(
        _mm_kernel,
        grid=(pl.cdiv(T, rb), N // tn),
        in_specs=[pl.BlockSpec((rb, K), lambda i, j: (i, 0)),
                  pl.BlockSpec((K, tn), lambda i, j: (0, j))],
        out_specs=pl.BlockSpec((rb, tn), lambda i, j: (i, j)),
        out_shape=jax.ShapeDtypeStruct((T, N), out_dtype),
        compiler_params=_params(("parallel", "arbitrary")),
        name=name,
    )(x, w)


def _halo_specs(tb, cb, n_in_blocks, n16, col_of):
    r = tb // HALO
    main = pl.BlockSpec((1, tb, cb), lambda b, i, c: (b, jnp.minimum(i, n_in_blocks - 1), col_of(c)))
    prev = pl.BlockSpec((1, HALO, cb), lambda b, i, c: (b, jnp.clip(i * r - 1, 0, n16 - 1), col_of(c)))
    nxt = pl.BlockSpec((1, HALO, cb), lambda b, i, c: (b, jnp.clip((i + 1) * r, 0, n16 - 1), col_of(c)))
    return [main, prev, nxt]


def _halo_window(xm, xp, xn, i, tb, seq_len):
    w = jnp.concatenate([xp[0], xm[0], xn[0]], axis=0).astype(F32)
    t = i * tb - HALO + lax.broadcasted_iota(I32, (tb + 2 * HALO, 1), 0)
    return jnp.where((t >= 0) & (t < seq_len), w, 0.0)


def _rglru_gate_kernel(xm, xp, xn, cw, cb, wa, ba, wx, bx, lam, a_ref, u_ref, *, tb, seq_len):
    i = pl.program_id(1)
    w = _halo_window(xm, xp, xn, i, tb, seq_len)
    o = HALO
    xb = (cw[0:1] * w[o - 1:o - 1 + tb] + cw[1:2] * w[o:o + tb]
          + cw[2:3] * w[o + 1:o + 1 + tb] + cw[3:4] * w[o + 2:o + 2 + tb] + cb[...])
    xbb = xb.astype(BF16)
    for d in range(2):
        r = jax.nn.sigmoid(jnp.dot(xbb, wa[d, 0], preferred_element_type=F32) + ba[d:d + 1])
        ig = jax.nn.sigmoid(jnp.dot(xbb, wx[d, 0], preferred_element_type=F32) + bx[d:d + 1])
        z = -lam[d:d + 1]
        softplus = jnp.maximum(z, 0.0) + jnp.log(1.0 + jnp.exp(-jnp.abs(z)))
        log_a = -RG_C * r * softplus
        a = jnp.exp(log_a)
        u = jnp.sqrt(1.0 - jnp.exp(2.0 * log_a)) * (ig * xb)
        a_ref[d, 0] = a
        u_ref[d, 0] = u


def _rglru_gates(pa3, conv_w, conv_b, wa, ba, wx, bx, lam):
    B, L, _ = pa3.shape
    DR = conv_w.shape[1]
    nk = DR // LANES
    tb = ROW_BLOCK
    nt = pl.cdiv(L, tb)
    n16 = L // HALO
    vec = lambda rows: pl.BlockSpec((rows, LANES), lambda b, i, k: (0, k))
    wspec = pl.BlockSpec((2, 1, LANES, LANES), lambda b, i, k: (0, k, 0, 0))
    out = pl.BlockSpec((2, 1, tb, LANES), lambda b, i, k: (0, b, i, k))
    return pl.pallas_call(
        functools.partial(_rglru_gate_kernel, tb=tb, seq_len=L),
        grid=(B, nt, nk),
        in_specs=_halo_specs(tb, LANES, nt, n16, lambda k: k)
        + [vec(4), vec(1), wspec, vec(2), wspec, vec(2), vec(2)],
        out_specs=[out, out],
        out_shape=[jax.ShapeDtypeStruct((2, B, L, DR), F32)] * 2,
        compiler_params=_params(("parallel", "parallel", "parallel")),
        name="rglru_gates",
    )(pa3, pa3, pa3, conv_w, conv_b.reshape(1, DR), wa, ba, wx, bx, lam)


def _scan_kernel(a_ref, u_ref, o_ref, h_ref, *, steps):
    d = pl.program_id(0)
    c = pl.program_id(1)

    @pl.when(c == 0)
    def _():
        h_ref[...] = jnp.zeros_like(h_ref)

    def body(i, h):
        t = i + d * (steps - 1 - 2 * i)
        h = a_ref[0, :, t] * h + u_ref[0, :, t]
        o_ref[0, :, t] = h
        return h

    h_ref[...] = lax.fori_loop(0, steps, body, h_ref[...], unroll=4)


def _rglru_scan(a, u):
    _, B, L, DR = a.shape
    r = DR // LANES
    a5 = a.reshape(2, B, L, r, LANES)
    u5 = u.reshape(2, B, L, r, LANES)
    budget = 2 * 1024 * 1024 // (B * r * LANES * 4)
    steps = max(s for s in range(1, L + 1) if L % s == 0 and s <= max(budget, 1))
    nch = L // steps
    spec = pl.BlockSpec((1, B, steps, r, LANES), lambda d, c: (d, 0, c + d * (nch - 1 - 2 * c), 0, 0))
    h = pl.pallas_call(
        functools.partial(_scan_kernel, steps=steps),
        grid=(2, nch),
        in_specs=[spec, spec],
        out_specs=spec,
        out_shape=jax.ShapeDtypeStruct((2, B, L, r, LANES), F32),
        scratch_shapes=[pltpu.VMEM((B, r, LANES), F32)],
        compiler_params=_params(("arbitrary", "arbitrary")),
        name="rglru_scan",
    )(a5, u5)
    return h.reshape(2, B * L, DR)


class _FftPlan:
    def __init__(self, seq_len):
        self.nb = -(-seq_len // HY_MAX_BLOCK)
        self.ns = _round_up(-(-seq_len // self.nb), 16)
        best = None
        for n1 in (30, 62, 94, 126):
            n2 = _round_up(-(-(2 * self.ns - 1) // n1), 16)
            cost = n1 * n2 * (n1 + n2)
            if best is None or cost < best[0]:
                best = (cost, n1, n2)
        _, self.n1, self.n2 = best
        self.n = self.n1 * self.n2
        self.k1 = self.n1 // 2 + 1
        self.j1 = _round_up(-(-self.ns // self.n2), 16)
        self.j1f = _round_up(self.n1, 16)
        self.rows = self.k1 * self.n2

    def tables(self):
        n1, n2, n, k1n = self.n1, self.n2, self.n, self.k1
        k1 = np.arange(k1n, dtype=np.float64)[None, :, None]
        j2 = np.arange(n2, dtype=np.float64)[:, None, None]

        def fwd(jcols):
            j1 = np.arange(jcols, dtype=np.float64)[None, None, :]
            ang = 2.0 * np.pi * k1 * (j1 * n2 + j2) / n
            live = (j1 < n1)
            return np.concatenate([np.cos(ang) * live, -np.sin(ang) * live], axis=1)

        g_sig = fwd(self.j1)
        g_filt = fwd(self.j1f)
        a = np.arange(n2, dtype=np.float64)
        phi = 2.0 * np.pi * np.outer(a, a) / n2
        c, s = np.cos(phi), np.sin(phi)
        m2f_r = np.concatenate([c, -s], axis=0)
        m2f_i = np.concatenate([s, c], axis=0)
        m2i_r = np.concatenate([c, s], axis=0)
        m2i_i = np.concatenate([-s, c], axis=0)
        j1 = np.arange(self.j1, dtype=np.float64)[None, :, None]
        kk = np.arange(k1n, dtype=np.float64)[None, None, :]
        ang = 2.0 * np.pi * kk * (j1 * n2 + j2) / n
        coef = np.where((kk == 0) | (kk == n1 // 2), 1.0, 2.0) / n
        gi_r = coef * np.cos(ang)
        gi_i = -coef * np.sin(ang)
        bf = lambda x: jnp.asarray(x, dtype=F32).astype(BF16)
        return dict(g_sig=bf(g_sig), g_filt=bf(g_filt), m2f_r=bf(m2f_r), m2f_i=bf(m2f_i),
                    m2i_r=bf(m2i_r), m2i_i=bf(m2i_i), gi_r=bf(gi_r), gi_i=bf(gi_i))


def _hy_conv3_kernel(*refs, tb, seq_len):
    x_refs = refs[0:9]
    cw = refs[9:12]
    cb = refs[12:15]
    s_ref, x0_ref = refs[15], refs[16]
    i = pl.program_id(1)
    o = HALO
    outs = []
    for p in range(3):
        w = _halo_window(x_refs[3 * p], x_refs[3 * p + 1], x_refs[3 * p + 2], i, tb, seq_len)
        outs.append(cw[p][0:1] * w[o - 1:o - 1 + tb] + cw[p][1:2] * w[o:o + tb]
                    + cw[p][2:3] * w[o + 1:o + 1 + tb] + cb[p][...])
    t = i * tb + lax.broadcasted_iota(I32, (tb, 1), 0)
    live = t < seq_len
    s_ref[0] = jnp.where(live, outs[1] * outs[2], 0.0)
    x0_ref[0] = jnp.where(live, outs[0], 0.0).astype(BF16)


def _hy_conv3(pb3, conv_w, conv_b, plan):
    B, L, _ = pb3.shape
    DH = conv_w.shape[1] // 3
    cb_ = _pick_tile(DH, (256, 128))
    nc = DH // cb_
    tb = ROW_BLOCK
    lp = plan.nb * plan.ns
    nt_in = pl.cdiv(L, tb)
    n16 = L // HALO
    in_specs = []
    for p in range(3):
        in_specs += _halo_specs(tb, cb_, nt_in, n16, lambda c, p=p: p * nc + c)
    in_specs += [pl.BlockSpec((3, cb_), lambda b, i, c, p=p: (0, p * nc + c)) for p in range(3)]
    in_specs += [pl.BlockSpec((1, cb_), lambda b, i, c, p=p: (0, p * nc + c)) for p in range(3)]
    out = pl.BlockSpec((1, tb, cb_), lambda b, i, c: (b, i, c))
    return pl.pallas_call(
        functools.partial(_hy_conv3_kernel, tb=tb, seq_len=L),
        grid=(B, pl.cdiv(lp, tb), nc),
        in_specs=in_specs,
        out_specs=[out, out],
        out_shape=[jax.ShapeDtypeStruct((B, lp, DH), F32), jax.ShapeDtypeStruct((B, lp, DH), BF16)],
        compiler_params=_params(("parallel", "parallel", "parallel")),
        name="hyena_conv3",
    )(*([pb3] * 9), *([conv_w] * 3), *([conv_b.reshape(1, 3 * DH)] * 3))


def _hy_lag_table(seq_len, plan):
    half = plan.nb * plan.ns
    rows = _round_up(2 * half, ROW_BLOCK)
    lag = np.arange(rows) - half
    pos = np.abs(lag)
    valid = (np.arange(rows) < 2 * half) & (pos <= seq_len - 1)
    posc = np.minimum(pos, seq_len - 1)
    t01 = np.linspace(0.0, 1.0, seq_len, dtype=np.float32)
    ang = ((2.0 * math.pi / seq_len) * np.arange(seq_len, dtype=np.float32)).astype(np.float32)
    bands = np.linspace(1e-4, HY_BANDS - 1, HY_BANDS, dtype=np.float32)
    arg = (bands[None, :] * ang[:, None]).astype(np.float64)
    emb = np.concatenate([t01[:, None].astype(np.float64), np.cos(arg), -np.sin(arg)], axis=-1)
    table = np.zeros((rows, LANES), np.float32)
    ne = emb.shape[1]
    table[:, :ne] = emb[posc] * valid[:, None]
    table[:, ne] = valid
    table[:, ne + 1] = lag >= 0
    return jnp.asarray(table), ne


def _hy_taps_kernel(e_ref, w1, b1, w2, b2, w3, fr, rates, hl_ref, norm_ref, *, ne, dh):
    hi = lax.Precision.HIGHEST
    e = e_ref[...]
    f = fr[...]
    z = jnp.sin(f * (jnp.dot(e, w1[...], preferred_element_type=F32, precision=hi) + b1[...]))
    z = jnp.sin(f * (jnp.dot(z, w2[...], preferred_element_type=F32, precision=hi) + b2[...]))
    z3 = jnp.dot(z, w3[...], preferred_element_type=F32, precision=hi)
    t01 = e[:, 0:1]
    valid = e[:, ne:ne + 1]
    is_fwd = e[:, ne + 1:ne + 2]
    decay = jnp.exp(-t01 * rates[...])
    taps = jnp.where(is_fwd > 0.5, z3[:, :dh], z3[:, dh:]) * decay * valid
    hl_ref[...] = taps

    @pl.when(pl.program_id(0) == 0)
    def _():
        norm_ref[...] = jnp.zeros_like(norm_ref)

    norm_ref[0:1, :] += jnp.sum(jnp.abs(taps), axis=0, keepdims=True)


def _hy_taps(seq_len, plan, w1, b1, w2, b2, w3, freq):
    table, ne = _hy_lag_table(seq_len, plan)
    rows = table.shape[0]
    nf = w1.shape[1]
    dh = w3.shape[1] // 2
    w1p = jnp.zeros((LANES, nf), F32).at[:ne].set(w1)
    rates = np.abs(np.linspace(math.log(HY_TARGET) / HY_SLOW_DECAY, math.log(HY_TARGET) / HY_FAST_DECAY,
                               dh, dtype=np.float32)).reshape(1, dh)
    rb = ROW_BLOCK
    full = lambda a: pl.BlockSpec(a.shape, lambda i: (0,) * a.ndim)
    args = (table, w1p, b1.reshape(1, nf), w2, b2.reshape(1, nf), w3, freq.reshape(1, nf), jnp.asarray(rates))
    return pl.pallas_call(
        functools.partial(_hy_taps_kernel, ne=ne, dh=dh),
        grid=(rows // rb,),
        in_specs=[pl.BlockSpec((rb, LANES), lambda i: (i, 0))] + [full(a) for a in args[1:]],
        out_specs=[pl.BlockSpec((rb, dh), lambda i: (i, 0)), pl.BlockSpec((8, dh), lambda i: (0, 0))],
        out_shape=[jax.ShapeDtypeStruct((rows, dh), F32), jax.ShapeDtypeStruct((8, dh), F32)],
        compiler_params=_params(("arbitrary",)),
        name="hyena_taps",
    )(*args)


def _dft_forward(src_ref, g_ref, m2r_ref, m2i_ref, xr, xi, out_r, out_i, scale, plan, jcols):
    n2, k1n = plan.n2, plan.k1

    def stage1(j2, c):
        v = src_ref[pl.ds(j2, jcols, stride=n2), :].astype(BF16)
        p = jnp.dot(g_ref[j2], v, preferred_element_type=F32)
        xr[pl.ds(j2, k1n, stride=n2), :] = p[:k1n]
        xi[pl.ds(j2, k1n, stride=n2), :] = p[k1n:]
        return c

    lax.fori_loop(0, n2, stage1, 0)

    def stage2(k, c):
        st = pl.multiple_of(k * n2, 16)
        yr = xr[pl.ds(st, n2), :].astype(BF16)
        yi = xi[pl.ds(st, n2), :].astype(BF16)
        z = (jnp.dot(m2r_ref[...], yr, preferred_element_type=F32)
             + jnp.dot(m2i_ref[...], yi, preferred_element_type=F32))
        if scale is not None:
            z = z * scale
        out_r[0, pl.ds(st, n2), :] = z[:n2].astype(BF16)
        out_i[0, pl.ds(st, n2), :] = z[n2:].astype(BF16)
        return c

    lax.fori_loop(0, k1n, stage2, 0)


def _hy_filter_fft_kernel(hi_ref, lo_ref, norm_ref, g_ref, m2r_ref, m2i_ref, hr_ref, hi_out_ref,
                          fbuf, xr, xi, *, plan):
    ns, n = plan.ns, plan.n
    rows = fbuf.shape[0]
    fbuf[pl.ds(0, ns), :] = hi_ref[...]
    fbuf[pl.ds(ns, rows - ns), :] = jnp.zeros((rows - ns, fbuf.shape[1]), F32)
    fbuf[pl.ds(n - ns, ns), :] = lo_ref[...]
    fbuf[pl.ds(n - ns, 1), :] = jnp.zeros((1, fbuf.shape[1]), F32)
    inv_norm = 1.0 / norm_ref[0:1, :]
    _dft_forward(fbuf, g_ref, m2r_ref, m2i_ref, xr, xi, hr_ref, hi_out_ref, inv_norm, plan, plan.j1f)


def _hy_filter_fft(hl, norm, plan, tabs):
    dh = hl.shape[1]
    cb_ = LANES
    nbd = 2 * plan.nb - 1
    ns = plan.ns
    spec_out = pl.BlockSpec((1, plan.rows, cb_), lambda p, c: (p, 0, c))
    return pl.pallas_call(
        functools.partial(_hy_filter_fft_kernel, plan=plan),
        grid=(nbd, dh // cb_),
        in_specs=[pl.BlockSpec((ns, cb_), lambda p, c: (p + 1, c)),
                  pl.BlockSpec((ns, cb_), lambda p, c: (p, c)),
                  pl.BlockSpec((8, cb_), lambda p, c: (0, c)),
                  _const_spec(tabs["g_filt"].shape), _const_spec(tabs["m2f_r"].shape),
                  _const_spec(tabs["m2f_i"].shape)],
        out_specs=[spec_out, spec_out],
        out_shape=[jax.ShapeDtypeStruct((nbd, plan.rows, dh), BF16)] * 2,
        scratch_shapes=[pltpu.VMEM((plan.j1f * plan.n2, cb_), F32),
                        pltpu.VMEM((plan.rows, cb_), F32), pltpu.VMEM((plan.rows, cb_), F32)],
        compiler_params=_params(("parallel", "parallel")),
        name="hyena_filter_fft",
    )(hl, hl, norm, tabs["g_filt"], tabs["m2f_r"], tabs["m2f_i"])


def _hy_fwd_fft_kernel(s_ref, g_ref, m2r_ref, m2i_ref, xr_out, xi_out, sbuf, xr, xi, *, plan):
    ns = plan.ns
    rows = sbuf.shape[0]
    sbuf[pl.ds(0, ns), :] = s_ref[0]
    if rows > ns:
        sbuf[pl.ds(ns, rows - ns), :] = jnp.zeros((rows - ns, sbuf.shape[1]), F32)
    _dft_forward(sbuf, g_ref, m2r_ref, m2i_ref, xr, xi, xr_out, xi_out, None, plan, plan.j1)


def _hy_fwd_fft(s_blocks, plan, tabs):
    q, ns, dh = s_blocks.shape
    cb_ = LANES
    spec_out = pl.BlockSpec((1, plan.rows, cb_), lambda b, c: (b, 0, c))
    return pl.pallas_call(
        functools.partial(_hy_fwd_fft_kernel, plan=plan),
        grid=(q, dh // cb_),
        in_specs=[pl.BlockSpec((1, ns, cb_), lambda b, c: (b, 0, c)),
                  _const_spec(tabs["g_sig"].shape), _const_spec(tabs["m2f_r"].shape),
                  _const_spec(tabs["m2f_i"].shape)],
        out_specs=[spec_out, spec_out],
        out_shape=[jax.ShapeDtypeStruct((q, plan.rows, dh), BF16)] * 2,
        scratch_shapes=[pltpu.VMEM((plan.j1 * plan.n2, cb_), F32),
                        pltpu.VMEM((plan.rows, cb_), F32), pltpu.VMEM((plan.rows, cb_), F32)],
        compiler_params=_params(("parallel", "parallel")),
        name="hyena_fwd_fft",
    )(s_blocks, tabs["g_sig"], tabs["m2f_r"], tabs["m2f_i"])


def _hy_mac_kernel(xr_ref, xi_ref, hr_ref, hi_ref, yr_ref, yi_ref, *, nb):
    for a in range(nb):
        yr = None
        for b in range(nb):
            d = a - b + nb - 1
            xr, xi = xr_ref[0, b].astype(F32), xi_ref[0, b].astype(F32)
            hr, hi = hr_ref[d].astype(F32), hi_ref[d].astype(F32)
            pr = xr * hr - xi * hi
            pi = xr * hi + xi * hr
            yr, yi = (pr, pi) if yr is None else (yr + pr, yi + pi)
        yr_ref[0, a] = yr.astype(BF16)
        yi_ref[0, a] = yi.astype(BF16)


def _hy_mac(xr, xi, hr, hi, batch, plan):
    nb = plan.nb
    dh = xr.shape[-1]
    rows = plan.rows
    rc = _pick_tile(rows, (256, 128, 64, 32, 16))
    cc = _pick_tile(dh, (512, 256, 128))
    x4 = lambda a: a.reshape(batch, nb, rows, dh)
    xs = pl.BlockSpec((1, nb, rc, cc), lambda b, r, c: (b, 0, r, c))
    hs = pl.BlockSpec((2 * nb - 1, rc, cc), lambda b, r, c: (0, r, c))
    yr, yi = pl.pallas_call(
        functools.partial(_hy_mac_kernel, nb=nb),
        grid=(batch, rows // rc, dh // cc),
        in_specs=[xs, xs, hs, hs],
        out_specs=[xs, xs],
        out_shape=[jax.ShapeDtypeStruct((batch, nb, rows, dh), BF16)] * 2,
        compiler_params=_params(("parallel", "parallel", "parallel")),
        name="hyena_freq_mac",
    )(x4(xr), x4(xi), hr, hi)
    return yr.reshape(batch * nb, rows, dh), yi.reshape(batch * nb, rows, dh)


def _hy_inv_fft_kernel(yr_ref, yi_ref, m2r_ref, m2i_ref, gr_ref, gi_ref, y_ref, ur, ui, ybuf, *, plan):
    n2, k1n, j1 = plan.n2, plan.k1, plan.j1

    def stage2(k, c):
        st = pl.multiple_of(k * n2, 16)
        w = (jnp.dot(m2r_ref[...], yr_ref[0, pl.ds(st, n2), :], preferred_element_type=F32)
             + jnp.dot(m2i_ref[...], yi_ref[0, pl.ds(st, n2), :], preferred_element_type=F32))
        ur[pl.ds(st, n2), :] = w[:n2]
        ui[pl.ds(st, n2), :] = w[n2:]
        return c

    lax.fori_loop(0, k1n, stage2, 0)

    def stage1(j2, c):
        vr = ur[pl.ds(j2, k1n, stride=n2), :].astype(BF16)
        vi = ui[pl.ds(j2, k1n, stride=n2), :].astype(BF16)
        y = (jnp.dot(gr_ref[j2], vr, preferred_element_type=F32)
             + jnp.dot(gi_ref[j2], vi, preferred_element_type=F32))
        ybuf[pl.ds(j2, j1, stride=n2), :] = y
        return c

    lax.fori_loop(0, n2, stage1, 0)
    y_ref[0] = ybuf[pl.ds(0, plan.ns), :]


def _hy_inv_fft(yr, yi, plan, tabs):
    q, rows, dh = yr.shape
    cb_ = LANES
    spec_in = pl.BlockSpec((1, rows, cb_), lambda b, c: (b, 0, c))
    return pl.pallas_call(
        functools.partial(_hy_inv_fft_kernel, plan=plan),
        grid=(q, dh // cb_),
        in_specs=[spec_in, spec_in,
                  _const_spec(tabs["m2i_r"].shape), _const_spec(tabs["m2i_i"].shape),
                  _const_spec(tabs["gi_r"].shape), _const_spec(tabs["gi_i"].shape)],
        out_specs=pl.BlockSpec((1, plan.ns, cb_), lambda b, c: (b, 0, c)),
        out_shape=jax.ShapeDtypeStruct((q, plan.ns, dh), F32),
        scratch_shapes=[pltpu.VMEM((rows, cb_), F32), pltpu.VMEM((rows, cb_), F32),
                        pltpu.VMEM((plan.j1 * plan.n2, cb_), F32)],
        compiler_params=_params(("parallel", "parallel")),
        name="hyena_inv_fft",
    )(yr, yi, tabs["m2i_r"], tabs["m2i_i"], tabs["gi_r"], tabs["gi_i"])


def _rope_tables(seq_len):
    half = QK_ROPE // 2
    inv = 1.0 / (ROPE_THETA ** (jnp.arange(0, QK_ROPE, 2, dtype=F32) / QK_ROPE))
    ang = jnp.arange(seq_len, dtype=F32)[:, None] * inv[None, :]
    cos2 = jnp.concatenate([jnp.cos(ang)] * 2, axis=-1)
    sin2 = jnp.concatenate([jnp.sin(ang)] * 2, axis=-1)
    scale = (QK_NOPE + QK_ROPE) ** -0.5
    zpad = jnp.zeros((seq_len, HEAD_PAD - QK_NOPE - QK_ROPE), F32)
    ones = jnp.ones((seq_len, QK_NOPE), F32)
    zer = jnp.zeros((seq_len, QK_NOPE), F32)
    zk = jnp.zeros((seq_len, LANES - QK_ROPE), F32)
    del half
    return jnp.concatenate([scale * ones, scale * cos2, zpad, zer, scale * sin2, zpad,
                            cos2, zk, sin2, zk], axis=-1)


def _mla_prep_kernel(pc_ref, tab_ref, gq_ref, gkv_ref, wa_ref, wb_ref, wkn_ref, wv_ref,
                     q_ref, k_ref, v_ref, *, ql, kvl):
    hp = HEAD_PAD
    pc = pc_ref[0].astype(F32)
    tab = tab_ref[...]
    qc, qs = tab[:, 0:hp], tab[:, hp:2 * hp]
    kc, ks = tab[:, 2 * hp:2 * hp + LANES], tab[:, 2 * hp + LANES:2 * hp + 2 * LANES]

    def rms(x, g):
        return (x * lax.rsqrt(jnp.mean(x * x, axis=-1, keepdims=True) + RMS_EPS) * g).astype(BF16)

    cq = rms(pc[:, 0:ql], gq_ref[...])
    ckv = rms(pc[:, ql:ql + kvl], gkv_ref[...])
    kr = pc[:, ql + kvl:ql + kvl + LANES] * kc + pc[:, ql + kvl + LANES:ql + kvl + 2 * LANES] * ks
    kr = kr.astype(BF16)
    for h in range(N_HEADS):
        qa = jnp.dot(cq, wa_ref[:, h * hp:(h + 1) * hp], preferred_element_type=F32)
        qb = jnp.dot(cq, wb_ref[:, h * hp:(h + 1) * hp], preferred_element_type=F32)
        q_ref[0, :, h * hp:(h + 1) * hp] = (qa * qc + qb * qs).astype(BF16)
        kn = jnp.dot(ckv, wkn_ref[:, h * QK_NOPE:(h + 1) * QK_NOPE], preferred_element_type=F32)
        k_ref[0, :, h * hp:h * hp + QK_NOPE] = kn.astype(BF16)
        k_ref[0, :, h * hp + QK_NOPE:(h + 1) * hp] = kr
        vv = jnp.dot(ckv, wv_ref[:, h * V_DIM:(h + 1) * V_DIM], preferred_element_type=F32)
        v_ref[0, :, h * V_DIM:(h + 1) * V_DIM] = vv.astype(BF16)


def _mla_prep(pc3, tab, gq, gkv, wa, wb, wkn, wv):
    B, L, pcw = pc3.shape
    ql, kvl = gq.shape[-1], gkv.shape[-1]
    tb = ROW_BLOCK
    hq = N_HEADS * HEAD_PAD
    hv = N_HEADS * V_DIM
    blk = lambda w: pl.BlockSpec((1, tb, w), lambda b, i: (b, i, 0))
    return pl.pallas_call(
        functools.partial(_mla_prep_kernel, ql=ql, kvl=kvl),
        grid=(B, pl.cdiv(L, tb)),
        in_specs=[blk(pcw), pl.BlockSpec((tb, tab.shape[1]), lambda b, i: (i, 0)),
                  _const_spec((1, ql)), _const_spec((1, kvl)),
                  _const_spec(wa.shape), _const_spec(wb.shape), _const_spec(wkn.shape), _const_spec(wv.shape)],
        out_specs=[blk(hq), blk(hq), blk(hv)],
        out_shape=[jax.ShapeDtypeStruct((B, L, hq), BF16), jax.ShapeDtypeStruct((B, L, hq), BF16),
                   jax.ShapeDtypeStruct((B, L, hv), BF16)],
        compiler_params=_params(("parallel", "parallel")),
        name="mla_prep",
    )(pc3, tab, gq.reshape(1, ql), gkv.reshape(1, kvl), wa, wb, wkn, wv)


def _attn_kernel(q_ref, k_ref, v_ref, o_ref, *, seq_len, ck):
    q = q_ref[0]
    tq = q.shape[0]

    def chunk(start, size, carry):
        m, l, acc = carry
        kc = k_ref[0, pl.ds(start, size), :]
        vc = v_ref[0, pl.ds(start, size), :]
        s = lax.dot_general(q, kc, (((1,), (1,)), ((), ())), preferred_element_type=F32)
        m_new = jnp.maximum(m, jnp.max(s, axis=-1, keepdims=True))
        alpha = jnp.exp(m - m_new)
        p = jnp.exp(s - m_new)
        l = alpha * l + jnp.sum(p, axis=-1, keepdims=True)
        acc = alpha * acc + jnp.dot(p.astype(BF16), vc, preferred_element_type=F32)
        return m_new, l, acc

    carry = (jnp.full((tq, 1), -1e30, F32), jnp.zeros((tq, 1), F32), jnp.zeros((tq, V_DIM), F32))
    rem = seq_len % ck
    if rem:
        carry = chunk(0, rem, carry)
    carry = lax.fori_loop(
        0, seq_len // ck, lambda c, cr: chunk(pl.multiple_of(rem + c * ck, 16), ck, cr), carry)
    _, l, acc = carry
    o_ref[0] = (acc / l).astype(BF16)


def _attention(q, k, v):
    B, L, _ = q.shape
    tq = ATT_TQ
    ck = min(ATT_CK, L)
    return pl.pallas_call(
        functools.partial(_attn_kernel, seq_len=L, ck=ck),
        grid=(B, N_HEADS, pl.cdiv(L, tq)),
        in_specs=[pl.BlockSpec((1, tq, HEAD_PAD), lambda b, h, i: (b, i, h)),
                  pl.BlockSpec((1, L, HEAD_PAD), lambda b, h, i: (b, 0, h)),
                  pl.BlockSpec((1, L, V_DIM), lambda b, h, i: (b, 0, h))],
        out_specs=pl.BlockSpec((1, tq, V_DIM), lambda b, h, i: (b, i, h)),
        out_shape=jax.ShapeDtypeStruct((B, L, N_HEADS * V_DIM), BF16),
        compiler_params=_params(("parallel", "parallel", "arbitrary")),
        name="mla_attention",
    )(q, k, v)


def _merge_kernel(h2_ref, gb_ref, y_ref, s_ref, x0_ref, skip_ref, o_ref, pg_ref,
                  wpa_ref, wpb_ref, wpc_ref, m_ref, *, d):
    gb = gb_ref[...].astype(F32)
    xa = (jax.nn.gelu(gb) * (h2_ref[0] + h2_ref[1])).astype(BF16)
    xb = (x0_ref[...].astype(F32) * (y_ref[...] + s_ref[...] * skip_ref[...])).astype(BF16)
    ya = jnp.dot(xa, wpa_ref[...], preferred_element_type=F32)
    yb = jnp.dot(xb, wpb_ref[...], preferred_element_type=F32)
    yc = jnp.dot(o_ref[...], wpc_ref[...], preferred_element_type=F32)
    g = jax.nn.sigmoid(pg_ref[...].astype(F32))
    m_ref[...] = (g[:, 0:d] * ya + g[:, d:2 * d] * yb + g[:, 2 * d:3 * d] * yc).astype(BF16)


def _merge(h2, pa, y, s, x0, skip, o, pg, wpa, wpb, wpc):
    T = o.shape[0]
    d = wpa.shape[1]
    dr, dh = wpa.shape[0], wpb.shape[0]
    rb = 256
    row = lambda w, col=0: pl.BlockSpec((rb, w), lambda i: (i, col))
    return pl.pallas_call(
        functools.partial(_merge_kernel, d=d),
        grid=(pl.cdiv(T, rb),),
        in_specs=[pl.BlockSpec((2, rb, dr), lambda i: (0, i, 0)), row(dr, 1), row(dh), row(dh), row(dh),
                  _const_spec((1, dh)), row(o.shape[1]), row(3 * d),
                  _const_spec(wpa.shape), _const_spec(wpb.shape), _const_spec(wpc.shape)],
        out_specs=row(d),
        out_shape=jax.ShapeDtypeStruct((T, d), BF16),
        compiler_params=_params(("parallel",)),
        name="branch_merge",
    )(h2, pa, y, s, x0, skip.reshape(1, dh), o, pg, wpa, wpb, wpc)


def _route(logits):
    ng, ne = N_GROUPS, N_GROUPS * EXPERTS_PER_GROUP
    lane = lax.broadcasted_iota(I32, logits.shape, 1)
    neg = jnp.float32(-jnp.inf)
    big = jnp.int32(4 * LANES)
    gl = jnp.where(lane < ng, logits, neg)
    gmax = jnp.max(gl, axis=-1, keepdims=True)
    g_sel = jnp.min(jnp.where(gl == gmax, lane, big), axis=-1, keepdims=True)
    p_sel = 1.0 / jnp.sum(jnp.exp(gl - gmax), axis=-1, keepdims=True)
    e_lane = lane - ng
    grp_shift = EXPERTS_PER_GROUP.bit_length() - 1
    in_grp = (e_lane >= 0) & (e_lane < ne) & (lax.shift_right_arithmetic(e_lane, grp_shift) == g_sel)
    el = jnp.where(in_grp, logits, neg)
    v0 = jnp.max(el, axis=-1, keepdims=True)
    i0 = jnp.min(jnp.where(el == v0, lane, big), axis=-1, keepdims=True)
    el1 = jnp.where(lane == i0, neg, el)
    v1 = jnp.max(el1, axis=-1, keepdims=True)
    i1 = jnp.min(jnp.where(el1 == v1, lane, big), axis=-1, keepdims=True)
    t = jnp.exp(v1 - v0)
    w0 = p_sel / (1.0 + t)
    w1 = p_sel * t / (1.0 + t)
    e0 = (i0 - ng).astype(F32)
    e1 = (i1 - ng).astype(F32)
    return jnp.where(lane == 0, e0, jnp.where(lane == 1, e1, jnp.where(lane == 2, w0, jnp.where(lane == 3, w1, 0.0))))


def _outproj_kernel(m_ref, h_ref, wout_ref, g_ref, b_ref, wr_hi_ref, wr_lo_ref, br_ref,
                    h1_ref, route_ref, *, alpha):
    mixed = jnp.dot(m_ref[...], wout_ref[...], preferred_element_type=F32)
    h1 = _layer_norm_val(alpha * h_ref[...] + mixed, g_ref[...], b_ref[...])
    h1_ref[...] = h1
    hi = h1.astype(BF16)
    lo = (h1 - hi.astype(F32)).astype(BF16)
    logits = (jnp.dot(hi, wr_hi_ref[...], preferred_element_type=F32)
              + jnp.dot(lo, wr_hi_ref[...], preferred_element_type=F32)
              + jnp.dot(hi, wr_lo_ref[...], preferred_element_type=F32)) + br_ref[...]
    route_ref[...] = _route(logits)


def _outproj(m, h, wout, g, b, wr_hi, wr_lo, br, alpha):
    T, d = h.shape
    rb = ROW_BLOCK
    row = lambda w: pl.BlockSpec((rb, w), lambda i: (i, 0))
    return pl.pallas_call(
        functools.partial(_outproj_kernel, alpha=alpha),
        grid=(pl.cdiv(T, rb),),
        in_specs=[row(d), row(d), _const_spec(wout.shape), _const_spec((1, d)), _const_spec((1, d)),
                  _const_spec(wr_hi.shape), _const_spec(wr_lo.shape), _const_spec((1, LANES))],
        out_specs=[row(d), row(LANES)],
        out_shape=[jax.ShapeDtypeStruct((T, d), F32), jax.ShapeDtypeStruct((T, LANES), F32)],
        compiler_params=_params(("parallel",)),
        name="outproj_ln_router",
    )(m, h, wout, g.reshape(1, d), b.reshape(1, d), wr_hi, wr_lo, br)


def _lane_cumsum(x):
    lane = lax.broadcasted_iota(I32, x.shape, 1)
    s = 1
    while s < x.shape[1]:
        x = x + jnp.where(lane >= s, pltpu.roll(x, s, axis=1), 0.0)
        s *= 2
    return x


def _assign_onehots(route, i, tb, tokens):
    lane = lax.broadcasted_iota(I32, (tb, LANES), 1)
    row = i * tb + lax.broadcasted_iota(I32, (tb, 1), 0)
    live = row < tokens
    e0 = jnp.where(live, route[:, 0:1], -1.0).astype(I32)
    e1 = jnp.where(live, route[:, 1:2], -1.0).astype(I32)
    oh0 = jnp.where(lane == e0, 1.0, 0.0)
    oh1 = jnp.where(lane == e1, 1.0, 0.0)
    return oh0, oh1


def _count_kernel(route_ref, counts_ref, *, tokens, tb):
    i = pl.program_id(0)
    oh0, oh1 = _assign_onehots(route_ref[...], i, tb, tokens)

    @pl.when(i == 0)
    def _():
        counts_ref[...] = jnp.zeros_like(counts_ref)

    counts_ref[0:1, :] += jnp.sum(oh0 + oh1, axis=0, keepdims=True)


def _rank_kernel(route_ref, counts_ref, dest_ref, blk_ref, base, *, tokens, tb, blk, nblocks):
    i = pl.program_id(0)
    ne = N_GROUPS * EXPERTS_PER_GROUP
    lane = lax.broadcasted_iota(I32, (tb, LANES), 1)
    oh0, oh1 = _assign_onehots(route_ref[...], i, tb, tokens)
    both = oh0 + oh1

    @pl.when(i == 0)
    def _():
        lane1 = lax.broadcasted_iota(I32, (1, LANES), 1)
        cnt = jnp.where(lane1 < ne, counts_ref[0:1, :], 0.0)
        padded = jnp.ceil(cnt / blk) * blk
        pend = _lane_cumsum(padded)
        base[...] = pend - padded
        jrow = lax.broadcasted_iota(I32, (nblocks, LANES), 0).astype(F32) * blk
        lanej = lax.broadcasted_iota(I32, (nblocks, LANES), 1)
        ended = jnp.where((lanej < ne) & (pend <= jrow), 1.0, 0.0)
        bexp = jnp.minimum(jnp.sum(ended, axis=-1, keepdims=True), ne - 1.0)
        used = jnp.sum(jnp.where(lane1 == ne - 1, pend, 0.0), axis=-1, keepdims=True) / blk
        blk_ref[...] = jnp.where(lanej == 0, bexp, jnp.where(lanej == 1, used, 0.0))

    r_i = lax.broadcasted_iota(I32, (tb, tb), 0)
    c_i = lax.broadcasted_iota(I32, (tb, tb), 1)
    ltri = jnp.where(c_i < r_i, 1.0, 0.0).astype(BF16)
    prefix = jnp.dot(ltri, both.astype(BF16), preferred_element_type=F32)
    slot = base[...] + prefix
    d0 = jnp.sum(oh0 * slot, axis=-1, keepdims=True)
    d1 = jnp.sum(oh1 * slot, axis=-1, keepdims=True)
    dest_ref[...] = jnp.where(lane == 0, d0, jnp.where(lane == 1, d1, 0.0))
    base[...] += jnp.sum(both, axis=0, keepdims=True)


def _dispatch(route, blk):
    T = route.shape[0]
    tb = ROW_BLOCK
    ne = N_GROUPS * EXPERTS_PER_GROUP
    nblocks = (2 * T + ne * (blk - 1) + blk - 1) // blk
    nbp = _round_up(nblocks, 8)
    rspec = pl.BlockSpec((tb, LANES), lambda i: (i, 0))
    cspec = pl.BlockSpec((8, LANES), lambda i: (0, 0))
    counts = pl.pallas_call(
        functools.partial(_count_kernel, tokens=T, tb=tb),
        grid=(pl.cdiv(T, tb),),
        in_specs=[rspec],
        out_specs=cspec,
        out_shape=jax.ShapeDtypeStruct((8, LANES), F32),
        compiler_params=_params(("arbitrary",)),
        name="moe_counts",
    )(route)
    dest, blkinfo = pl.pallas_call(
        functools.partial(_rank_kernel, tokens=T, tb=tb, blk=blk, nblocks=nbp),
        grid=(pl.cdiv(T, tb),),
        in_specs=[rspec, cspec],
        out_specs=[rspec, pl.BlockSpec((nbp, LANES), lambda i: (0, 0))],
        out_shape=[jax.ShapeDtypeStruct((T, LANES), F32), jax.ShapeDtypeStruct((nbp, LANES), F32)],
        scratch_shapes=[pltpu.VMEM((1, LANES), F32)],
        compiler_params=_params(("arbitrary",)),
        name="moe_rank",
    )(route, counts)
    return dest, blkinfo, nblocks


def _scatter_kernel(dest_ref, x_ref, xs_in, xs_out, sem, *, tb):
    del xs_in

    def copy(r, s):
        d = dest_ref[0, 0, 2 * r + s]
        return d, pltpu.make_async_copy(x_ref.at[pl.ds(r, 1)], xs_out.at[pl.ds(jnp.maximum(d, 0), 1)], sem)

    def start(r, c):
        for s in range(2):
            d, cp = copy(r, s)

            @pl.when(d >= 0)
            def _():
                cp.start()
        return c

    def wait(r, c):
        for s in range(2):
            d, cp = copy(r, s)

            @pl.when(d >= 0)
            def _():
                cp.wait()
        return c

    lax.fori_loop(0, tb, start, 0)
    lax.fori_loop(0, tb, wait, 0)


def _moe_scatter(h1, dest_idx, rows, tb):
    T, d = h1.shape
    nblk = dest_idx.shape[0]
    xs0 = jnp.zeros((rows, d), F32)
    return pl.pallas_call(
        functools.partial(_scatter_kernel, tb=tb),
        grid=(nblk,),
        in_specs=[pl.BlockSpec((1, 1, 2 * tb), lambda i: (i, 0, 0), memory_space=pltpu.SMEM),
                  pl.BlockSpec((tb, d), lambda i: (i, 0)),
                  pl.BlockSpec(memory_space=pl.ANY)],
        out_specs=pl.BlockSpec(memory_space=pl.ANY),
        out_shape=jax.ShapeDtypeStruct((rows, d), F32),
        scratch_shapes=[pltpu.SemaphoreType.DMA(())],
        input_output_aliases={2: 0},
        compiler_params=_params(("arbitrary",)),
        name="moe_scatter",
    )(dest_idx, h1, xs0)


def _expert_kernel(be_ref, nu_ref, x_ref, wg_ref, wu_ref, wd_ref, y_ref):
    i = pl.program_id(0)

    @pl.when(i < nu_ref[0])
    def _():
        x = x_ref[...].astype(BF16)
        g = jnp.dot(x, wg_ref[0], preferred_element_type=F32)
        u = jnp.dot(x, wu_ref[0], preferred_element_type=F32)
        mid = (g * jax.nn.sigmoid(g) * u).astype(BF16)
        y_ref[...] = jnp.dot(mid, wd_ref[0], preferred_element_type=F32)

    @pl.when(i >= nu_ref[0])
    def _():
        y_ref[...] = jnp.zeros_like(y_ref)


def _moe_experts(xs, blk_e, n_used, wg, wu, wd, blk):
    rows, d = xs.shape
    de = wg.shape[-1]
    nblocks = rows // blk
    grid_spec = pltpu.PrefetchScalarGridSpec(
        num_scalar_prefetch=2,
        grid=(nblocks,),
        in_specs=[pl.BlockSpec((blk, d), lambda i, be, nu: (i, 0)),
                  pl.BlockSpec((1, d, de), lambda i, be, nu: (be[i], 0, 0)),
                  pl.BlockSpec((1, d, de), lambda i, be, nu: (be[i], 0, 0)),
                  pl.BlockSpec((1, de, d), lambda i, be, nu: (be[i], 0, 0))],
        out_specs=pl.BlockSpec((blk, d), lambda i, be, nu: (i, 0)),
    )
    return pl.pallas_call(
        _expert_kernel,
        grid_spec=grid_spec,
        out_shape=jax.ShapeDtypeStruct((rows, d), F32),
        compiler_params=_params(("arbitrary",)),
        name="moe_experts",
    )(blk_e, n_used, xs, wg, wu, wd)


def _combine_kernel(dest_ref, h_ref, route_ref, g_ref, b_ref, ys_hbm, o_ref, ob_ref, gbuf, sem, *, tb, alpha):
    def copy(r, s):
        d = dest_ref[0, 0, 2 * r + s]
        return pltpu.make_async_copy(ys_hbm.at[pl.ds(d, 1)], gbuf.at[s, pl.ds(r, 1)], sem)

    def start(r, c):
        copy(r, 0).start()
        copy(r, 1).start()
        return c

    def wait(r, c):
        copy(r, 0).wait()
        copy(r, 1).wait()
        return c

    lax.fori_loop(0, tb, start, 0)
    lax.fori_loop(0, tb, wait, 0)
    route = route_ref[...]
    moe = route[:, 2:3] * gbuf[0] + route[:, 3:4] * gbuf[1]
    y = _layer_norm_val(alpha * h_ref[...] + moe, g_ref[...], b_ref[...])
    o_ref[...] = y
    ob_ref[...] = y.astype(BF16)


def _moe_combine(h1, route, dest_idx, ys, g, b, alpha, tb):
    T, d = h1.shape
    nblk = dest_idx.shape[0]
    row = lambda w: pl.BlockSpec((tb, w), lambda i: (i, 0))
    return pl.pallas_call(
        functools.partial(_combine_kernel, tb=tb, alpha=alpha),
        grid=(nblk,),
        in_specs=[pl.BlockSpec((1, 1, 2 * tb), lambda i: (i, 0, 0), memory_space=pltpu.SMEM),
                  row(d), row(LANES), _const_spec((1, d)), _const_spec((1, d)),
                  pl.BlockSpec(memory_space=pl.ANY)],
        out_specs=[row(d), row(d)],
        out_shape=[jax.ShapeDtypeStruct((T, d), F32), jax.ShapeDtypeStruct((T, d), BF16)],
        scratch_shapes=[pltpu.VMEM((2, tb, d), F32), pltpu.SemaphoreType.DMA(())],
        compiler_params=_params(("arbitrary",)),
        name="moe_combine_ln",
    )(dest_idx, h1, route, g.reshape(1, d), b.reshape(1, d), ys)


def _moe(h1, route, wg, wu, wd, g, b, alpha):
    T, d = h1.shape
    blk = MOE_BLOCK
    tb = 256
    dest, blkinfo, nblocks = _dispatch(route, blk)
    nblk_tok = pl.cdiv(T, tb)
    di = dest[:, 0:2].astype(I32)
    pad = nblk_tok * tb - T
    scat_idx = jnp.pad(di, ((0, pad), (0, 0)), constant_values=-1).reshape(nblk_tok, 1, 2 * tb)
    gath_idx = jnp.pad(di, ((0, pad), (0, 0)), constant_values=0).reshape(nblk_tok, 1, 2 * tb)
    blk_e = blkinfo[:nblocks, 0].astype(I32)
    n_used = blkinfo[0:1, 1].astype(I32)
    xs = _moe_scatter(h1, scat_idx, nblocks * blk, tb)
    ys = _moe_experts(xs, blk_e, n_used, wg, wu, wd, blk)
    return _moe_combine(h1, route, gath_idx, ys, g, b, alpha, tb)


def _prep_weights(W):
    depth, d, _ = W["w_in"].shape
    dr = W["conv_a_w"].shape[-1]
    dh = W["conv_b_w"].shape[-1] // 3
    ql, kvl = W["q_norm_g"].shape[-1], W["kv_norm_g"].shape[-1]
    ca, cbw = 2 * dr, 3 * dh
    cc = ql + kvl + QK_ROPE
    w_in = W["w_in"]
    half = QK_ROPE // 2

    def rot(w):
        return jnp.concatenate([-w[..., half:], w[..., :half]], axis=-1)

    w_a = w_in[..., :ca]
    w_b = w_in[..., ca:ca + cbw]
    w_c = w_in[..., ca + cbw:ca + cbw + cc]
    w_g = w_in[..., ca + cbw + cc:]
    w_kr = w_c[..., ql + kvl:]
    zpad = jnp.zeros(w_kr.shape[:-1] + (LANES - QK_ROPE,), F32)
    w_c_ext = jnp.concatenate([w_c[..., :ql + kvl], w_kr, zpad, rot(w_kr), zpad], axis=-1)

    wq = W["w_uq"].reshape(depth, ql, N_HEADS, QK_NOPE + QK_ROPE)
    z_n = jnp.zeros((depth, ql, N_HEADS, QK_NOPE), F32)
    z_p = jnp.zeros((depth, ql, N_HEADS, HEAD_PAD - QK_NOPE - QK_ROPE), F32)
    wa = jnp.concatenate([wq, z_p], axis=-1).reshape(depth, ql, N_HEADS * HEAD_PAD)
    wb = jnp.concatenate([z_n, rot(wq[..., QK_NOPE:]), z_p], axis=-1).reshape(depth, ql, N_HEADS * HEAD_PAD)
    wkv = W["w_ukv"].reshape(depth, kvl, N_HEADS, QK_NOPE + V_DIM)
    wkn = wkv[..., :QK_NOPE].reshape(depth, kvl, N_HEADS * QK_NOPE)
    wv = wkv[..., QK_NOPE:].reshape(depth, kvl, N_HEADS * V_DIM)

    ng, ne = N_GROUPS, N_GROUPS * EXPERTS_PER_GROUP
    wr = jnp.concatenate([W["w_rg"], W["w_re"], jnp.zeros((depth, d, LANES - ng - ne), F32)], axis=-1)
    br = jnp.concatenate([W["b_rg"], W["b_re"], jnp.zeros((depth, LANES - ng - ne), F32)], axis=-1)
    wr_hi = wr.astype(BF16)
    wr_lo = (wr - wr_hi.astype(F32)).astype(BF16)
    bf = lambda x: x.astype(BF16)
    return dict(
        w_a=bf(w_a), w_b=bf(w_b), w_c=bf(w_c_ext), w_g=bf(w_g),
        conv_a_w=W["conv_a_w"], conv_a_b=W["conv_a_b"], rg_wa=bf(W["rg_wa"]), rg_ba=W["rg_ba"],
        rg_wx=bf(W["rg_wx"]), rg_bx=W["rg_bx"], rg_lambda=W["rg_lambda"],
        conv_b_w=W["conv_b_w"], conv_b_b=W["conv_b_b"], hy_w1=W["hy_w1"], hy_b1=W["hy_b1"],
        hy_w2=W["hy_w2"], hy_b2=W["hy_b2"], hy_w3=W["hy_w3"], hy_freq=W["hy_freq"], hy_skip=W["hy_skip"],
        q_norm_g=W["q_norm_g"], kv_norm_g=W["kv_norm_g"], wa=bf(wa), wb=bf(wb), wkn=bf(wkn), wv=bf(wv),
        w_pa=bf(W["w_pa"]), w_pb=bf(W["w_pb"]), w_pc=bf(W["w_pc"]), w_out=bf(W["w_out"]),
        ln1_g=W["ln1_g"], ln1_b=W["ln1_b"], wr_hi=wr_hi, wr_lo=wr_lo, br=br.reshape(depth, 1, LANES),
        w_e_gate=bf(W["w_e_gate"]), w_e_up=bf(W["w_e_up"]), w_e_down=bf(W["w_e_down"]),
        ln2_g=W["ln2_g"], ln2_b=W["ln2_b"],
    )


def _encoder_layer(h, hb, P, B, L, plan, tabs, rope_tab, alpha):
    T = B * L
    pa = _matmul(hb, P["w_a"], BF16, "in_proj_a")
    pb = _matmul(hb, P["w_b"], BF16, "in_proj_b")
    pc = _matmul(hb, P["w_c"], BF16, "in_proj_c")
    pg = _matmul(hb, P["w_g"], BF16, "in_proj_g")

    a, u = _rglru_gates(pa.reshape(B, L, -1), P["conv_a_w"], P["conv_a_b"], P["rg_wa"], P["rg_ba"],
                        P["rg_wx"], P["rg_bx"], P["rg_lambda"])
    h2 = _rglru_scan(a, u)

    s, x0 = _hy_conv3(pb.reshape(B, L, -1), P["conv_b_w"], P["conv_b_b"], plan)
    dh = s.shape[-1]
    hl, norm = _hy_taps(L, plan, P["hy_w1"], P["hy_b1"], P["hy_w2"], P["hy_b2"], P["hy_w3"], P["hy_freq"])
    f_r, f_i = _hy_filter_fft(hl, norm, plan, tabs)
    x_r, x_i = _hy_fwd_fft(s.reshape(B * plan.nb, plan.ns, dh), plan, tabs)
    y_r, y_i = _hy_mac(x_r, x_i, f_r, f_i, B, plan)
    y = _hy_inv_fft(y_r, y_i, plan, tabs).reshape(B, plan.nb * plan.ns, dh)
    if plan.nb * plan.ns != L and B > 1:
        y, s, x0 = y[:, :L], s[:, :L], x0[:, :L]
    flat = lambda t: t.reshape(-1, dh)

    q, k, v = _mla_prep(pc.reshape(B, L, -1), rope_tab, P["q_norm_g"], P["kv_norm_g"],
                        P["wa"], P["wb"], P["wkn"], P["wv"])
    o = _attention(q, k, v).reshape(T, -1)

    m = _merge(h2, pa, flat(y), flat(s), flat(x0), P["hy_skip"], o, pg, P["w_pa"], P["w_pb"], P["w_pc"])
    h1, route = _outproj(m, h, P["w_out"], P["ln1_g"], P["ln1_b"], P["wr_hi"], P["wr_lo"], P["br"], alpha)
    return _moe(h1, route, P["w_e_gate"], P["w_e_up"], P["w_e_down"], P["ln2_g"], P["ln2_b"], alpha)


def _encoder_trunk(x, meta, ln_g, ln_b, PW, alpha):
    B, S, D = x.shape
    L = S + N_META
    xc = jnp.concatenate([jnp.broadcast_to(meta[None], (B, N_META, D)), x], axis=1).reshape(B * L, D)
    h, hb = _layer_norm_rows(xc, ln_g, ln_b)
    plan = _FftPlan(L)
    tabs = plan.tables()
    rope_tab = _rope_tables(L)

    def body(carry, P):
        return _encoder_layer(carry[0], carry[1], P, B, L, plan, tabs, rope_tab, alpha), None

    (h, hb), _ = lax.scan(body, (h, hb), PW)
    return h.reshape(B, L, D)[:, N_META:]


def kernel(x_prompt, x_sample, meta, ln_emb_g, ln_emb_b, w_in, conv_a_w, conv_a_b, rg_wa, rg_ba, rg_wx, rg_bx, rg_lambda, conv_b_w, conv_b_b, hy_w1, hy_b1, hy_w2, hy_b2, hy_w3, hy_freq, hy_skip, q_norm_g, w_uq, kv_norm_g, w_ukv, w_pa, w_pb, w_pc, w_out, ln1_g, ln1_b, w_rg, b_rg, w_re, b_re, w_e_gate, w_e_up, w_e_down, ln2_g, ln2_b):
    W = dict(w_in=w_in, conv_a_w=conv_a_w, conv_a_b=conv_a_b, rg_wa=rg_wa, rg_ba=rg_ba, rg_wx=rg_wx,
             rg_bx=rg_bx, rg_lambda=rg_lambda, conv_b_w=conv_b_w, conv_b_b=conv_b_b, hy_w1=hy_w1,
             hy_b1=hy_b1, hy_w2=hy_w2, hy_b2=hy_b2, hy_w3=hy_w3, hy_freq=hy_freq, hy_skip=hy_skip,
             q_norm_g=q_norm_g, w_uq=w_uq, kv_norm_g=kv_norm_g, w_ukv=w_ukv, w_pa=w_pa, w_pb=w_pb,
             w_pc=w_pc, w_out=w_out, ln1_g=ln1_g, ln1_b=ln1_b, w_rg=w_rg, b_rg=b_rg, w_re=w_re, b_re=b_re,
             w_e_gate=w_e_gate, w_e_up=w_e_up, w_e_down=w_e_down, ln2_g=ln2_g, ln2_b=ln2_b)
    depth = w_in.shape[0]
    alpha = (2 * depth) ** 0.25
    PW = _prep_weights(W)
    y_prompt = _encoder_trunk(x_prompt, meta, ln_emb_g, ln_emb_b, PW, alpha)
    y_sample = _encoder_trunk(x_sample, meta, ln_emb_g, ln_emb_b, PW, alpha)
    return (y_prompt, y_sample)
```

```python
import functools
import math

import numpy as np
import jax
import jax.numpy as jnp
from jax import lax
from jax.experimental import pallas as pl
from jax.experimental.pallas import tpu as pltpu

F32 = jnp.float32
BF16 = jnp.bfloat16
I32 = jnp.int32

N_META = 16
RG_C = 8.0
HY_BANDS = 16
HY_FAST_DECAY = 0.3
HY_SLOW_DECAY = 1.5
HY_TARGET = 1e-2
N_HEADS = 16
QK_NOPE = 128
QK_ROPE = 64
V_DIM = 128
ROPE_THETA = 10000.0
N_GROUPS = 8
EXPERTS_PER_GROUP = 8
LN_EPS = 1e-5
RMS_EPS = 1e-6

V7X_VMEM_BYTES = 64 * 1024 * 1024
VMEM_LIMIT = 56 * 1024 * 1024
LANES = 128
HEAD_PAD = 2 * LANES

ROW_BLOCK = 512
MM_ROW_BLOCK = 1024
HALO = 16
HY_MAX_BLOCK = 4608
MOE_BLOCK = 256
ATT_TQ = 384
ATT_CK = 512
ATT_UNROLL = 8
ATT_MIN_DENOM = 2.0 ** -60
DFT_UNROLL = 8


def _params(sem, vmem=VMEM_LIMIT):
    return pltpu.CompilerParams(dimension_semantics=sem, vmem_limit_bytes=vmem)


def _round_up(x, m):
    return (x + m - 1) // m * m


def _pick_tile(n, cands):
    for c in cands:
        if n % c == 0:
            return c
    return n


def _const_spec(shape):
    nd = len(shape)
    return pl.BlockSpec(shape, lambda *a: (0,) * nd, pipeline_mode=pl.Buffered(1))


def _layer_norm_val(x, g, b):
    mu = jnp.mean(x, axis=-1, keepdims=True)
    xc = x - mu
    var = jnp.mean(xc * xc, axis=-1, keepdims=True)
    return xc * lax.rsqrt(var + LN_EPS) * g + b


def _ln_kernel(x_ref, g_ref, b_ref, o_ref, ob_ref):
    y = _layer_norm_val(x_ref[...], g_ref[...], b_ref[...])
    o_ref[...] = y
    ob_ref[...] = y.astype(BF16)


def _layer_norm_rows(x, g, b):
    T, D = x.shape
    rb = ROW_BLOCK
    row = pl.BlockSpec((rb, D), lambda i: (i, 0))
    vec = pl.BlockSpec((1, D), lambda i: (0, 0))
    return pl.pallas_call(
        _ln_kernel,
        grid=(pl.cdiv(T, rb),),
        in_specs=[row, vec, vec],
        out_specs=[row, row],
        out_shape=[jax.ShapeDtypeStruct((T, D), F32), jax.ShapeDtypeStruct((T, D), BF16)],
        compiler_params=_params(("parallel",)),
        name="embed_ln",
    )(x, g.reshape(1, D), b.reshape(1, D))


def _mm_kernel(x_ref, w_ref, o_ref):
    o_ref[...] = jnp.dot(x_ref[...], w_ref[...], preferred_element_type=F32).astype(o_ref.dtype)


def _matmul(x, w, out_dtype, name):
    T, K = x.shape
    N = w.shape[1]
    rb = MM_ROW_BLOCK
    tn = _pick_tile(N, (512, 384, 256, 128))
    return pl.pallas_call(
        _mm_kernel,
        grid=(pl.cdiv(T, rb), N // tn),
        in_specs=[pl.BlockSpec((rb, K), lambda i, j: (i, 0)),
                  pl.BlockSpec((K, tn), lambda i, j: (0, j))],
        out_specs=pl.BlockSpec((rb, tn), lambda i, j: (i, j)),
        out_shape=jax.ShapeDtypeStruct((T, N), out_dtype),
        compiler_params=_params(("parallel", "arbitrary")),
        name=name,
    )(x, w)


def _halo_specs(tb, cb, n_in_blocks, n16, col_of):
    r = tb // HALO
    main = pl.BlockSpec((1, tb, cb), lambda b, i, c: (b, jnp.minimum(i, n_in_blocks - 1), col_of(c)))
    prev = pl.BlockSpec((1, HALO, cb), lambda b, i, c: (b, jnp.clip(i * r - 1, 0, n16 - 1), col_of(c)))
    nxt = pl.BlockSpec((1, HALO, cb), lambda b, i, c: (b, jnp.clip((i + 1) * r, 0, n16 - 1), col_of(c)))
    return [main, prev, nxt]


def _halo_window(xm, xp, xn, i, tb, seq_len):
    w = jnp.concatenate([xp[0], xm[0], xn[0]], axis=0).astype(F32)
    t = i * tb - HALO + lax.broadcasted_iota(I32, (tb + 2 * HALO, 1), 0)
    return jnp.where((t >= 0) & (t < seq_len), w, 0.0)


def _rglru_gate_kernel(xm, xp, xn, cw, cb, wa, ba, wx, bx, lam, a_ref, u_ref, *, tb, seq_len):
    i = pl.program_id(1)
    w = _halo_window(xm, xp, xn, i, tb, seq_len)
    o = HALO
    xb = (cw[0:1] * w[o - 1:o - 1 + tb] + cw[1:2] * w[o:o + tb]
          + cw[2:3] * w[o + 1:o + 1 + tb] + cw[3:4] * w[o + 2:o + 2 + tb] + cb[...])
    xbb = xb.astype(BF16)
    for d in range(2):
        r = jax.nn.sigmoid(jnp.dot(xbb, wa[d, 0], preferred_element_type=F32) + ba[d:d + 1])
        ig = jax.nn.sigmoid(jnp.dot(xbb, wx[d, 0], preferred_element_type=F32) + bx[d:d + 1])
        z = -lam[d:d + 1]
        softplus = jnp.maximum(z, 0.0) + jnp.log(1.0 + jnp.exp(-jnp.abs(z)))
        log_a = -RG_C * r * softplus
        a = jnp.exp(log_a)
        u = jnp.sqrt(1.0 - jnp.exp(2.0 * log_a)) * (ig * xb)
        a_ref[d, 0] = a
        u_ref[d, 0] = u


def _rglru_gates(pa3, conv_w, conv_b, wa, ba, wx, bx, lam):
    B, L, _ = pa3.shape
    DR = conv_w.shape[1]
    nk = DR // LANES
    tb = ROW_BLOCK
    nt = pl.cdiv(L, tb)
    n16 = L // HALO
    vec = lambda rows: pl.BlockSpec((rows, LANES), lambda b, i, k: (0, k))
    wspec = pl.BlockSpec((2, 1, LANES, LANES), lambda b, i, k: (0, k, 0, 0))
    out = pl.BlockSpec((2, 1, tb, LANES), lambda b, i, k: (0, b, i, k))
    return pl.pallas_call(
        functools.partial(_rglru_gate_kernel, tb=tb, seq_len=L),
        grid=(B, nt, nk),
        in_specs=_halo_specs(tb, LANES, nt, n16, lambda k: k)
        + [vec(4), vec(1), wspec, vec(2), wspec, vec(2), vec(2)],
        out_specs=[out, out],
        out_shape=[jax.ShapeDtypeStruct((2, B, L, DR), F32)] * 2,
        compiler_params=_params(("parallel", "parallel", "parallel")),
        name="rglru_gates",
    )(pa3, pa3, pa3, conv_w, conv_b.reshape(1, DR), wa, ba, wx, bx, lam)


def _scan_kernel(a_ref, u_ref, o_ref, h_ref, *, steps):
    d = pl.program_id(0)
    c = pl.program_id(1)

    @pl.when(c == 0)
    def _():
        h_ref[...] = jnp.zeros_like(h_ref)

    def body(i, h):
        t = i + d * (steps - 1 - 2 * i)
        h = a_ref[0, :, t] * h + u_ref[0, :, t]
        o_ref[0, :, t] = h
        return h

    h_ref[...] = lax.fori_loop(0, steps, body, h_ref[...], unroll=4)


def _rglru_scan(a, u):
    _, B, L, DR = a.shape
    r = DR // LANES
    a5 = a.reshape(2, B, L, r, LANES)
    u5 = u.reshape(2, B, L, r, LANES)
    budget = 2 * 1024 * 1024 // (B * r * LANES * 4)
    steps = max(s for s in range(1, L + 1) if L % s == 0 and s <= max(budget, 1))
    nch = L // steps
    spec = pl.BlockSpec((1, B, steps, r, LANES), lambda d, c: (d, 0, c + d * (nch - 1 - 2 * c), 0, 0))
    h = pl.pallas_call(
        functools.partial(_scan_kernel, steps=steps),
        grid=(2, nch),
        in_specs=[spec, spec],
        out_specs=spec,
        out_shape=jax.ShapeDtypeStruct((2, B, L, r, LANES), F32),
        scratch_shapes=[pltpu.VMEM((B, r, LANES), F32)],
        compiler_params=_params(("arbitrary", "arbitrary")),
        name="rglru_scan",
    )(a5, u5)
    return h.reshape(2, B * L, DR)


class _FftPlan:
    def __init__(self, seq_len):
        self.nb = -(-seq_len // HY_MAX_BLOCK)
        self.ns = _round_up(-(-seq_len // self.nb), 16)
        best = None
        for n1 in (30, 62, 94, 126):
            n2 = _round_up(-(-(2 * self.ns - 1) // n1), 16)
            cost = n1 * n2 * (n1 + n2)
            if best is None or cost < best[0]:
                best = (cost, n1, n2)
        _, self.n1, self.n2 = best
        self.n = self.n1 * self.n2
        self.k1 = self.n1 // 2 + 1
        self.j1 = _round_up(-(-self.ns // self.n2), 16)
        self.j1f = _round_up(self.n1, 16)
        self.rows = self.k1 * self.n2

    def tables(self):
        n1, n2, n, k1n = self.n1, self.n2, self.n, self.k1
        k1 = np.arange(k1n, dtype=np.float64)[None, :, None]
        j2 = np.arange(n2, dtype=np.float64)[:, None, None]

        def fwd(jcols):
            j1 = np.arange(jcols, dtype=np.float64)[None, None, :]
            ang = 2.0 * np.pi * k1 * (j1 * n2 + j2) / n
            live = (j1 < n1)
            return np.concatenate([np.cos(ang) * live, -np.sin(ang) * live], axis=1)

        g_sig = fwd(self.j1)
        g_filt = fwd(self.j1f)
        a = np.arange(n2, dtype=np.float64)
        phi = 2.0 * np.pi * np.outer(a, a) / n2
        c, s = np.cos(phi), np.sin(phi)
        m2f_r = np.concatenate([c, -s], axis=0)
        m2f_i = np.concatenate([s, c], axis=0)
        m2i_r = np.concatenate([c, s], axis=0)
        m2i_i = np.concatenate([-s, c], axis=0)
        j1 = np.arange(self.j1, dtype=np.float64)[None, :, None]
        kk = np.arange(k1n, dtype=np.float64)[None, None, :]
        ang = 2.0 * np.pi * kk * (j1 * n2 + j2) / n
        coef = np.where((kk == 0) | (kk == n1 // 2), 1.0, 2.0) / n
        gi_r = coef * np.cos(ang)
        gi_i = -coef * np.sin(ang)
        bf = lambda x: jnp.asarray(x, dtype=F32).astype(BF16)
        return dict(g_sig=bf(g_sig), g_filt=bf(g_filt), m2f_r=bf(m2f_r), m2f_i=bf(m2f_i),
                    m2i_r=bf(m2i_r), m2i_i=bf(m2i_i), gi_r=bf(gi_r), gi_i=bf(gi_i))


def _hy_conv3_kernel(*refs, tb, seq_len):
    x_refs = refs[0:9]
    cw = refs[9:12]
    cb = refs[12:15]
    s_ref, x0_ref = refs[15], refs[16]
    i = pl.program_id(1)
    o = HALO
    outs = []
    for p in range(3):
        w = _halo_window(x_refs[3 * p], x_refs[3 * p + 1], x_refs[3 * p + 2], i, tb, seq_len)
        outs.append(cw[p][0:1] * w[o - 1:o - 1 + tb] + cw[p][1:2] * w[o:o + tb]
                    + cw[p][2:3] * w[o + 1:o + 1 + tb] + cb[p][...])
    t = i * tb + lax.broadcasted_iota(I32, (tb, 1), 0)
    live = t < seq_len
    s_ref[0] = jnp.where(live, outs[1] * outs[2], 0.0)
    x0_ref[0] = jnp.where(live, outs[0], 0.0).astype(BF16)


def _hy_conv3(pb3, conv_w, conv_b, plan):
    B, L, _ = pb3.shape
    DH = conv_w.shape[1] // 3
    cb_ = _pick_tile(DH, (256, 128))
    nc = DH // cb_
    tb = ROW_BLOCK
    lp = plan.nb * plan.ns
    nt_in = pl.cdiv(L, tb)
    n16 = L // HALO
    in_specs = []
    for p in range(3):
        in_specs += _halo_specs(tb, cb_, nt_in, n16, lambda c, p=p: p * nc + c)
    in_specs += [pl.BlockSpec((3, cb_), lambda b, i, c, p=p: (0, p * nc + c)) for p in range(3)]
    in_specs += [pl.BlockSpec((1, cb_), lambda b, i, c, p=p: (0, p * nc + c)) for p in range(3)]
    out = pl.BlockSpec((1, tb, cb_), lambda b, i, c: (b, i, c))
    return pl.pallas_call(
        functools.partial(_hy_conv3_kernel, tb=tb, seq_len=L),
        grid=(B, pl.cdiv(lp, tb), nc),
        in_specs=in_specs,
        out_specs=[out, out],
        out_shape=[jax.ShapeDtypeStruct((B, lp, DH), F32), jax.ShapeDtypeStruct((B, lp, DH), BF16)],
        compiler_params=_params(("parallel", "parallel", "parallel")),
        name="hyena_conv3",
    )(*([pb3] * 9), *([conv_w] * 3), *([conv_b.reshape(1, 3 * DH)] * 3))


def _hy_lag_table(seq_len, plan):
    half = plan.nb * plan.ns
    rows = _round_up(2 * half, ROW_BLOCK)
    lag = np.arange(rows) - half
    pos = np.abs(lag)
    valid = (np.arange(rows) < 2 * half) & (pos <= seq_len - 1)
    posc = np.minimum(pos, seq_len - 1)
    t01 = np.linspace(0.0, 1.0, seq_len, dtype=np.float32)
    ang = ((2.0 * math.pi / seq_len) * np.arange(seq_len, dtype=np.float32)).astype(np.float32)
    bands = np.linspace(1e-4, HY_BANDS - 1, HY_BANDS, dtype=np.float32)
    arg = (bands[None, :] * ang[:, None]).astype(np.float64)
    emb = np.concatenate([t01[:, None].astype(np.float64), np.cos(arg), -np.sin(arg)], axis=-1)
    table = np.zeros((rows, LANES), np.float32)
    ne = emb.shape[1]
    table[:, :ne] = emb[posc] * valid[:, None]
    table[:, ne] = valid
    table[:, ne + 1] = lag >= 0
    return jnp.asarray(table), ne


def _hy_taps_kernel(e_ref, w1, b1, w2, b2, w3, fr, rates, hl_ref, norm_ref, *, ne, dh):
    hi = lax.Precision.HIGHEST
    e = e_ref[...]
    f = fr[...]
    z = jnp.sin(f * (jnp.dot(e, w1[...], preferred_element_type=F32, precision=hi) + b1[...]))
    z = jnp.sin(f * (jnp.dot(z, w2[...], preferred_element_type=F32, precision=hi) + b2[...]))
    z3 = jnp.dot(z, w3[...], preferred_element_type=F32, precision=hi)
    t01 = e[:, 0:1]
    valid = e[:, ne:ne + 1]
    is_fwd = e[:, ne + 1:ne + 2]
    decay = jnp.exp(-t01 * rates[...])
    taps = jnp.where(is_fwd > 0.5, z3[:, :dh], z3[:, dh:]) * decay * valid
    hl_ref[...] = taps

    @pl.when(pl.program_id(0) == 0)
    def _():
        norm_ref[...] = jnp.zeros_like(norm_ref)

    norm_ref[0:1, :] += jnp.sum(jnp.abs(taps), axis=0, keepdims=True)


def _hy_taps(seq_len, plan, w1, b1, w2, b2, w3, freq):
    table, ne = _hy_lag_table(seq_len, plan)
    rows = table.shape[0]
    nf = w1.shape[1]
    dh = w3.shape[1] // 2
    w1p = jnp.zeros((LANES, nf), F32).at[:ne].set(w1)
    rates = np.abs(np.linspace(math.log(HY_TARGET) / HY_SLOW_DECAY, math.log(HY_TARGET) / HY_FAST_DECAY,
                               dh, dtype=np.float32)).reshape(1, dh)
    rb = ROW_BLOCK
    full = lambda a: pl.BlockSpec(a.shape, lambda i: (0,) * a.ndim)
    args = (table, w1p, b1.reshape(1, nf), w2, b2.reshape(1, nf), w3, freq.reshape(1, nf), jnp.asarray(rates))
    return pl.pallas_call(
        functools.partial(_hy_taps_kernel, ne=ne, dh=dh),
        grid=(rows // rb,),
        in_specs=[pl.BlockSpec((rb, LANES), lambda i: (i, 0))] + [full(a) for a in args[1:]],
        out_specs=[pl.BlockSpec((rb, dh), lambda i: (i, 0)), pl.BlockSpec((8, dh), lambda i: (0, 0))],
        out_shape=[jax.ShapeDtypeStruct((rows, dh), F32), jax.ShapeDtypeStruct((8, dh), F32)],
        compiler_params=_params(("arbitrary",)),
        name="hyena_taps",
    )(*args)


def _dft_forward(src_ref, g_ref, m2r_ref, m2i_ref, xr, xi, out_r, out_i, scale, plan, jcols):
    n2, k1n = plan.n2, plan.k1

    def stage1(j2, c):
        v = src_ref[pl.ds(j2, jcols, stride=n2), :].astype(BF16)
        p = jnp.dot(g_ref[j2], v, preferred_element_type=F32)
        xr[pl.ds(j2, k1n, stride=n2), :] = p[:k1n]
        xi[pl.ds(j2, k1n, stride=n2), :] = p[k1n:]
        return c

    lax.fori_loop(0, n2, stage1, 0, unroll=DFT_UNROLL)

    def stage2(k, c):
        st = pl.multiple_of(k * n2, 16)
        yr = xr[pl.ds(st, n2), :].astype(BF16)
        yi = xi[pl.ds(st, n2), :].astype(BF16)
        z = (jnp.dot(m2r_ref[...], yr, preferred_element_type=F32)
             + jnp.dot(m2i_ref[...], yi, preferred_element_type=F32))
        if scale is not None:
            z = z * scale
        out_r[0, pl.ds(st, n2), :] = z[:n2].astype(BF16)
        out_i[0, pl.ds(st, n2), :] = z[n2:].astype(BF16)
        return c

    lax.fori_loop(0, k1n, stage2, 0, unroll=DFT_UNROLL)


def _hy_filter_fft_kernel(hi_ref, lo_ref, norm_ref, g_ref, m2r_ref, m2i_ref, hr_ref, hi_out_ref,
                          fbuf, xr, xi, *, plan):
    ns, n = plan.ns, plan.n
    rows = fbuf.shape[0]
    fbuf[pl.ds(0, ns), :] = hi_ref[...]
    fbuf[pl.ds(ns, rows - ns), :] = jnp.zeros((rows - ns, fbuf.shape[1]), F32)
    fbuf[pl.ds(n - ns, ns), :] = lo_ref[...]
    fbuf[pl.ds(n - ns, 1), :] = jnp.zeros((1, fbuf.shape[1]), F32)
    inv_norm = 1.0 / norm_ref[0:1, :]
    _dft_forward(fbuf, g_ref, m2r_ref, m2i_ref, xr, xi, hr_ref, hi_out_ref, inv_norm, plan, plan.j1f)


def _hy_filter_fft(hl, norm, plan, tabs):
    dh = hl.shape[1]
    cb_ = LANES
    nbd = 2 * plan.nb - 1
    ns = plan.ns
    spec_out = pl.BlockSpec((1, plan.rows, cb_), lambda p, c: (p, 0, c))
    return pl.pallas_call(
        functools.partial(_hy_filter_fft_kernel, plan=plan),
        grid=(nbd, dh // cb_),
        in_specs=[pl.BlockSpec((ns, cb_), lambda p, c: (p + 1, c)),
                  pl.BlockSpec((ns, cb_), lambda p, c: (p, c)),
                  pl.BlockSpec((8, cb_), lambda p, c: (0, c)),
                  _const_spec(tabs["g_filt"].shape), _const_spec(tabs["m2f_r"].shape),
                  _const_spec(tabs["m2f_i"].shape)],
        out_specs=[spec_out, spec_out],
        out_shape=[jax.ShapeDtypeStruct((nbd, plan.rows, dh), BF16)] * 2,
        scratch_shapes=[pltpu.VMEM((plan.j1f * plan.n2, cb_), F32),
                        pltpu.VMEM((plan.rows, cb_), F32), pltpu.VMEM((plan.rows, cb_), F32)],
        compiler_params=_params(("parallel", "parallel")),
        name="hyena_filter_fft",
    )(hl, hl, norm, tabs["g_filt"], tabs["m2f_r"], tabs["m2f_i"])


def _hy_fwd_fft_kernel(s_ref, g_ref, m2r_ref, m2i_ref, xr_out, xi_out, sbuf, xr, xi, *, plan):
    ns = plan.ns
    rows = sbuf.shape[0]
    sbuf[pl.ds(0, ns), :] = s_ref[0]
    if rows > ns:
        sbuf[pl.ds(ns, rows - ns), :] = jnp.zeros((rows - ns, sbuf.shape[1]), F32)
    _dft_forward(sbuf, g_ref, m2r_ref, m2i_ref, xr, xi, xr_out, xi_out, None, plan, plan.j1)


def _hy_fwd_fft(s_blocks, plan, tabs):
    q, ns, dh = s_blocks.shape
    cb_ = LANES
    spec_out = pl.BlockSpec((1, plan.rows, cb_), lambda b, c: (b, 0, c))
    return pl.pallas_call(
        functools.partial(_hy_fwd_fft_kernel, plan=plan),
        grid=(q, dh // cb_),
        in_specs=[pl.BlockSpec((1, ns, cb_), lambda b, c: (b, 0, c)),
                  _const_spec(tabs["g_sig"].shape), _const_spec(tabs["m2f_r"].shape),
                  _const_spec(tabs["m2f_i"].shape)],
        out_specs=[spec_out, spec_out],
        out_shape=[jax.ShapeDtypeStruct((q, plan.rows, dh), BF16)] * 2,
        scratch_shapes=[pltpu.VMEM((plan.j1 * plan.n2, cb_), F32),
                        pltpu.VMEM((plan.rows, cb_), F32), pltpu.VMEM((plan.rows, cb_), F32)],
        compiler_params=_params(("parallel", "parallel")),
        name="hyena_fwd_fft",
    )(s_blocks, tabs["g_sig"], tabs["m2f_r"], tabs["m2f_i"])


def _hy_mac_kernel(xr_ref, xi_ref, hr_ref, hi_ref, yr_ref, yi_ref, *, nb):
    for a in range(nb):
        yr = None
        for b in range(nb):
            d = a - b + nb - 1
            xr, xi = xr_ref[0, b].astype(F32), xi_ref[0, b].astype(F32)
            hr, hi = hr_ref[d].astype(F32), hi_ref[d].astype(F32)
            pr = xr * hr - xi * hi
            pi = xr * hi + xi * hr
            yr, yi = (pr, pi) if yr is None else (yr + pr, yi + pi)
        yr_ref[0, a] = yr.astype(BF16)
        yi_ref[0, a] = yi.astype(BF16)


def _hy_mac(xr, xi, hr, hi, batch, plan):
    nb = plan.nb
    dh = xr.shape[-1]
    rows = plan.rows
    rc = _pick_tile(rows, (256, 128, 64, 32, 16))
    cc = _pick_tile(dh, (512, 256, 128))
    x4 = lambda a: a.reshape(batch, nb, rows, dh)
    xs = pl.BlockSpec((1, nb, rc, cc), lambda b, r, c: (b, 0, r, c))
    hs = pl.BlockSpec((2 * nb - 1, rc, cc), lambda b, r, c: (0, r, c))
    yr, yi = pl.pallas_call(
        functools.partial(_hy_mac_kernel, nb=nb),
        grid=(batch, rows // rc, dh // cc),
        in_specs=[xs, xs, hs, hs],
        out_specs=[xs, xs],
        out_shape=[jax.ShapeDtypeStruct((batch, nb, rows, dh), BF16)] * 2,
        compiler_params=_params(("parallel", "parallel", "parallel")),
        name="hyena_freq_mac",
    )(x4(xr), x4(xi), hr, hi)
    return yr.reshape(batch * nb, rows, dh), yi.reshape(batch * nb, rows, dh)


def _hy_inv_fft_kernel(yr_ref, yi_ref, m2r_ref, m2i_ref, gr_ref, gi_ref, y_ref, ur, ui, ybuf, *, plan):
    n2, k1n, j1 = plan.n2, plan.k1, plan.j1

    def stage2(k, c):
        st = pl.multiple_of(k * n2, 16)
        w = (jnp.dot(m2r_ref[...], yr_ref[0, pl.ds(st, n2), :], preferred_element_type=F32)
             + jnp.dot(m2i_ref[...], yi_ref[0, pl.ds(st, n2), :], preferred_element_type=F32))
        ur[pl.ds(st, n2), :] = w[:n2]
        ui[pl.ds(st, n2), :] = w[n2:]
        return c

    lax.fori_loop(0, k1n, stage2, 0, unroll=DFT_UNROLL)

    def stage1(j2, c):
        vr = ur[pl.ds(j2, k1n, stride=n2), :].astype(BF16)
        vi = ui[pl.ds(j2, k1n, stride=n2), :].astype(BF16)
        y = (jnp.dot(gr_ref[j2], vr, preferred_element_type=F32)
             + jnp.dot(gi_ref[j2], vi, preferred_element_type=F32))
        ybuf[pl.ds(j2, j1, stride=n2), :] = y
        return c

    lax.fori_loop(0, n2, stage1, 0, unroll=DFT_UNROLL)
    y_ref[0] = ybuf[pl.ds(0, plan.ns), :]


def _hy_inv_fft(yr, yi, plan, tabs):
    q, rows, dh = yr.shape
    cb_ = LANES
    spec_in = pl.BlockSpec((1, rows, cb_), lambda b, c: (b, 0, c))
    return pl.pallas_call(
        functools.partial(_hy_inv_fft_kernel, plan=plan),
        grid=(q, dh // cb_),
        in_specs=[spec_in, spec_in,
                  _const_spec(tabs["m2i_r"].shape), _const_spec(tabs["m2i_i"].shape),
                  _const_spec(tabs["gi_r"].shape), _const_spec(tabs["gi_i"].shape)],
        out_specs=pl.BlockSpec((1, plan.ns, cb_), lambda b, c: (b, 0, c)),
        out_shape=jax.ShapeDtypeStruct((q, plan.ns, dh), F32),
        scratch_shapes=[pltpu.VMEM((rows, cb_), F32), pltpu.VMEM((rows, cb_), F32),
                        pltpu.VMEM((plan.j1 * plan.n2, cb_), F32)],
        compiler_params=_params(("parallel", "parallel")),
        name="hyena_inv_fft",
    )(yr, yi, tabs["m2i_r"], tabs["m2i_i"], tabs["gi_r"], tabs["gi_i"])


def _rope_tables(seq_len):
    half = QK_ROPE // 2
    inv = 1.0 / (ROPE_THETA ** (jnp.arange(0, QK_ROPE, 2, dtype=F32) / QK_ROPE))
    ang = jnp.arange(seq_len, dtype=F32)[:, None] * inv[None, :]
    cos2 = jnp.concatenate([jnp.cos(ang)] * 2, axis=-1)
    sin2 = jnp.concatenate([jnp.sin(ang)] * 2, axis=-1)
    scale = (QK_NOPE + QK_ROPE) ** -0.5 * math.log2(math.e)
    zpad = jnp.zeros((seq_len, HEAD_PAD - QK_NOPE - QK_ROPE), F32)
    ones = jnp.ones((seq_len, QK_NOPE), F32)
    zer = jnp.zeros((seq_len, QK_NOPE), F32)
    zk = jnp.zeros((seq_len, LANES - QK_ROPE), F32)
    del half
    return jnp.concatenate([scale * ones, scale * cos2, zpad, zer, scale * sin2, zpad,
                            cos2, zk, sin2, zk], axis=-1)


def _mla_prep_kernel(pc_ref, tab_ref, gq_ref, gkv_ref, wa_ref, wb_ref, wkn_ref, wv_ref,
                     q_ref, k_ref, v_ref, kmax_ref, *, ql, kvl, seq_len, tb):
    hp = HEAD_PAD
    i = pl.program_id(1)
    live = (i * tb + lax.broadcasted_iota(I32, (tb, 1), 0)) < seq_len
    lane8 = lax.broadcasted_iota(I32, (8, LANES), 1)

    @pl.when(i == 0)
    def _():
        kmax_ref[...] = jnp.zeros_like(kmax_ref)

    pc = pc_ref[0].astype(F32)
    tab = tab_ref[...]
    qc, qs = tab[:, 0:hp], tab[:, hp:2 * hp]
    kc, ks = tab[:, 2 * hp:2 * hp + LANES], tab[:, 2 * hp + LANES:2 * hp + 2 * LANES]

    def rms(x, g):
        return (x * lax.rsqrt(jnp.mean(x * x, axis=-1, keepdims=True) + RMS_EPS) * g).astype(BF16)

    cq = rms(pc[:, 0:ql], gq_ref[...])
    ckv = rms(pc[:, ql:ql + kvl], gkv_ref[...])
    kr = pc[:, ql + kvl:ql + kvl + LANES] * kc + pc[:, ql + kvl + LANES:ql + kvl + 2 * LANES] * ks
    kr = kr.astype(BF16)
    kr_sq = jnp.sum(jnp.square(kr.astype(F32)), axis=-1, keepdims=True)
    lane_t = lax.broadcasted_iota(I32, (tb, hp - V_DIM), 1)
    ones_col = jnp.where(lane_t == 0, 1.0, 0.0).astype(BF16)
    k_tail = jnp.where(lane_t == QK_ROPE, 1.0, kr.astype(F32)).astype(BF16)
    kmax = kmax_ref[0]
    for h in range(N_HEADS):
        qa = jnp.dot(cq, wa_ref[:, h * hp:(h + 1) * hp], preferred_element_type=F32)
        qb = jnp.dot(cq, wb_ref[:, h * hp:(h + 1) * hp], preferred_element_type=F32)
        q_ref[0, :, h * hp:(h + 1) * hp] = (qa * qc + qb * qs).astype(BF16)
        kn = jnp.dot(ckv, wkn_ref[:, h * QK_NOPE:(h + 1) * QK_NOPE], preferred_element_type=F32).astype(BF16)
        k_ref[0, :, h * hp:h * hp + QK_NOPE] = kn
        k_ref[0, :, h * hp + QK_NOPE:(h + 1) * hp] = k_tail
        k_sq = jnp.sum(jnp.square(kn.astype(F32)), axis=-1, keepdims=True) + kr_sq
        k_sq_max = jnp.max(jnp.where(live, k_sq, 0.0), axis=0, keepdims=True)
        kmax = jnp.where(lane8 == h, jnp.maximum(kmax, k_sq_max), kmax)
        vv = jnp.dot(ckv, wv_ref[:, h * V_DIM:(h + 1) * V_DIM], preferred_element_type=F32)
        v_ref[0, :, h * hp:h * hp + V_DIM] = vv.astype(BF16)
        v_ref[0, :, h * hp + V_DIM:(h + 1) * hp] = ones_col
    kmax_ref[0] = kmax


def _mla_prep(pc3, tab, gq, gkv, wa, wb, wkn, wv):
    B, L, pcw = pc3.shape
    ql, kvl = gq.shape[-1], gkv.shape[-1]
    tb = ROW_BLOCK
    hq = N_HEADS * HEAD_PAD
    hv = N_HEADS * HEAD_PAD
    blk = lambda w: pl.BlockSpec((1, tb, w), lambda b, i: (b, i, 0))
    return pl.pallas_call(
        functools.partial(_mla_prep_kernel, ql=ql, kvl=kvl, seq_len=L, tb=tb),
        grid=(B, pl.cdiv(L, tb)),
        in_specs=[blk(pcw), pl.BlockSpec((tb, tab.shape[1]), lambda b, i: (i, 0)),
                  _const_spec((1, ql)), _const_spec((1, kvl)),
                  _const_spec(wa.shape), _const_spec(wb.shape), _const_spec(wkn.shape), _const_spec(wv.shape)],
        out_specs=[blk(hq), blk(hq), blk(hv), pl.BlockSpec((1, 8, LANES), lambda b, i: (b, 0, 0))],
        out_shape=[jax.ShapeDtypeStruct((B, L, hq), BF16), jax.ShapeDtypeStruct((B, L, hq), BF16),
                   jax.ShapeDtypeStruct((B, L, hv), BF16), jax.ShapeDtypeStruct((B, 8, LANES), F32)],
        compiler_params=_params(("parallel", "arbitrary")),
        name="mla_prep",
    )(pc3, tab, gq.reshape(1, ql), gkv.reshape(1, kvl), wa, wb, wkn, wv)


def _attn_kernel(q_ref, k_ref, v_ref, kmax_ref, o_ref, *, seq_len, ck, unroll):
    h = pl.program_id(1)
    i = pl.program_id(2)
    q = q_ref[0]
    tq = q.shape[0]
    nt = (((1,), (1,)), ((), ()))
    full = seq_len // ck
    lead = seq_len % ck

    def over_chunks(step, state):
        if lead:
            state = step(0, lead, state)
        for c in range(full % unroll):
            state = step(lead + c * ck, ck, state)
        base = lead + (full % unroll) * ck

        def body(it, st):
            for c in range(unroll):
                st = step(pl.multiple_of(base + (it * unroll + c) * ck, 16), ck, st)
            return st

        return lax.fori_loop(0, full // unroll, body, state)

    qf = q.astype(F32)
    q_norm = jnp.sqrt(jnp.sum(qf * qf, axis=-1, keepdims=True))
    lane8 = lax.broadcasted_iota(I32, (8, LANES), 1)
    k_sq_max = jnp.max(jnp.where(lane8 == h, kmax_ref[0], 0.0), axis=(0, 1), keepdims=True)
    bound = q_norm * jnp.sqrt(k_sq_max)
    lane = lax.broadcasted_iota(I32, (tq, HEAD_PAD), 1)
    q_shift = jnp.where(lane == QK_NOPE + QK_ROPE, (-bound).astype(BF16), q)

    def fast(start, size, acc):
        s = lax.dot_general(q_shift, k_ref[0, pl.ds(start, size), :], nt, preferred_element_type=F32)
        p = jnp.exp2(s).astype(BF16)
        return acc + jnp.dot(p, v_ref[0, pl.ds(start, size), :], preferred_element_type=F32)

    acc = over_chunks(fast, jnp.zeros((tq, HEAD_PAD), F32))
    denom = acc[:, V_DIM:V_DIM + 1]
    o_ref[0] = (acc[:, :V_DIM] / denom).astype(BF16)

    rows = i * tq + lax.broadcasted_iota(I32, (tq, 1), 0)
    healthy = jnp.where(rows < seq_len, denom, 1.0) >= ATT_MIN_DENOM
    n_bad = jnp.sum(jnp.where(healthy, 0.0, 1.0))

    @pl.when(n_bad > 0.0)
    def _():
        def exact(start, size, state):
            m, acc_e = state
            s = lax.dot_general(q, k_ref[0, pl.ds(start, size), :], nt, preferred_element_type=F32)
            m_new = jnp.maximum(m, jnp.max(s, axis=-1, keepdims=True))
            p = jnp.exp2(s - m_new).astype(BF16)
            acc_e = jnp.exp2(m - m_new) * acc_e + jnp.dot(
                p, v_ref[0, pl.ds(start, size), :], preferred_element_type=F32)
            return m_new, acc_e

        _, acc_e = over_chunks(exact, (jnp.full((tq, 1), -1e30, F32), jnp.zeros((tq, HEAD_PAD), F32)))
        o_ref[0] = (acc_e[:, :V_DIM] / acc_e[:, V_DIM:V_DIM + 1]).astype(BF16)


def _attention(q, k, v, kmax):
    B, L, _ = q.shape
    tq = ATT_TQ
    ck = min(ATT_CK, L)
    return pl.pallas_call(
        functools.partial(_attn_kernel, seq_len=L, ck=ck, unroll=ATT_UNROLL),
        grid=(B, N_HEADS, pl.cdiv(L, tq)),
        in_specs=[pl.BlockSpec((1, tq, HEAD_PAD), lambda b, h, i: (b, i, h)),
                  pl.BlockSpec((1, L, HEAD_PAD), lambda b, h, i: (b, 0, h)),
                  pl.BlockSpec((1, L, HEAD_PAD), lambda b, h, i: (b, 0, h)),
                  pl.BlockSpec((1, 8, LANES), lambda b, h, i: (b, 0, 0))],
        out_specs=pl.BlockSpec((1, tq, V_DIM), lambda b, h, i: (b, i, h)),
        out_shape=jax.ShapeDtypeStruct((B, L, N_HEADS * V_DIM), BF16),
        compiler_params=_params(("parallel", "parallel", "arbitrary")),
        name="mla_attention",
    )(q, k, v, kmax)


def _merge_kernel(h2_ref, gb_ref, y_ref, s_ref, x0_ref, skip_ref, o_ref, pg_ref,
                  wpa_ref, wpb_ref, wpc_ref, m_ref, *, d):
    gb = gb_ref[...].astype(F32)
    xa = (jax.nn.gelu(gb) * (h2_ref[0] + h2_ref[1])).astype(BF16)
    xb = (x0_ref[...].astype(F32) * (y_ref[...] + s_ref[...] * skip_ref[...])).astype(BF16)
    ya = jnp.dot(xa, wpa_ref[...], preferred_element_type=F32)
    yb = jnp.dot(xb, wpb_ref[...], preferred_element_type=F32)
    yc = jnp.dot(o_ref[...], wpc_ref[...], preferred_element_type=F32)
    g = jax.nn.sigmoid(pg_ref[...].astype(F32))
    m_ref[...] = (g[:, 0:d] * ya + g[:, d:2 * d] * yb + g[:, 2 * d:3 * d] * yc).astype(BF16)


def _merge(h2, pa, y, s, x0, skip, o, pg, wpa, wpb, wpc):
    T = o.shape[0]
    d = wpa.shape[1]
    dr, dh = wpa.shape[0], wpb.shape[0]
    rb = 256
    row = lambda w, col=0: pl.BlockSpec((rb, w), lambda i: (i, col))
    return pl.pallas_call(
        functools.partial(_merge_kernel, d=d),
        grid=(pl.cdiv(T, rb),),
        in_specs=[pl.BlockSpec((2, rb, dr), lambda i: (0, i, 0)), row(dr, 1), row(dh), row(dh), row(dh),
                  _const_spec((1, dh)), row(o.shape[1]), row(3 * d),
                  _const_spec(wpa.shape), _const_spec(wpb.shape), _const_spec(wpc.shape)],
        out_specs=row(d),
        out_shape=jax.ShapeDtypeStruct((T, d), BF16),
        compiler_params=_params(("parallel",)),
        name="branch_merge",
    )(h2, pa, y, s, x0, skip.reshape(1, dh), o, pg, wpa, wpb, wpc)


def _route(logits):
    ng, ne = N_GROUPS, N_GROUPS * EXPERTS_PER_GROUP
    lane = lax.broadcasted_iota(I32, logits.shape, 1)
    neg = jnp.float32(-jnp.inf)
    big = jnp.int32(4 * LANES)
    gl = jnp.where(lane < ng, logits, neg)
    gmax = jnp.max(gl, axis=-1, keepdims=True)
    g_sel = jnp.min(jnp.where(gl == gmax, lane, big), axis=-1, keepdims=True)
    p_sel = 1.0 / jnp.sum(jnp.exp(gl - gmax), axis=-1, keepdims=True)
    e_lane = lane - ng
    grp_shift = EXPERTS_PER_GROUP.bit_length() - 1
    in_grp = (e_lane >= 0) & (e_lane < ne) & (lax.shift_right_arithmetic(e_lane, grp_shift) == g_sel)
    el = jnp.where(in_grp, logits, neg)
    v0 = jnp.max(el, axis=-1, keepdims=True)
    i0 = jnp.min(jnp.where(el == v0, lane, big), axis=-1, keepdims=True)
    el1 = jnp.where(lane == i0, neg, el)
    v1 = jnp.max(el1, axis=-1, keepdims=True)
    i1 = jnp.min(jnp.where(el1 == v1, lane, big), axis=-1, keepdims=True)
    t = jnp.exp(v1 - v0)
    w0 = p_sel / (1.0 + t)
    w1 = p_sel * t / (1.0 + t)
    e0 = (i0 - ng).astype(F32)
    e1 = (i1 - ng).astype(F32)
    return jnp.where(lane == 0, e0, jnp.where(lane == 1, e1, jnp.where(lane == 2, w0, jnp.where(lane == 3, w1, 0.0))))


def _outproj_kernel(m_ref, h_ref, wout_ref, g_ref, b_ref, wr_hi_ref, wr_lo_ref, br_ref,
                    h1_ref, route_ref, *, alpha):
    mixed = jnp.dot(m_ref[...], wout_ref[...], preferred_element_type=F32)
    h1 = _layer_norm_val(alpha * h_ref[...] + mixed, g_ref[...], b_ref[...])
    h1_ref[...] = h1
    hi = h1.astype(BF16)
    lo = (h1 - hi.astype(F32)).astype(BF16)
    logits = (jnp.dot(hi, wr_hi_ref[...], preferred_element_type=F32)
              + jnp.dot(lo, wr_hi_ref[...], preferred_element_type=F32)
              + jnp.dot(hi, wr_lo_ref[...], preferred_element_type=F32)) + br_ref[...]
    route_ref[...] = _route(logits)


def _outproj(m, h, wout, g, b, wr_hi, wr_lo, br, alpha):
    T, d = h.shape
    rb = ROW_BLOCK
    row = lambda w: pl.BlockSpec((rb, w), lambda i: (i, 0))
    return pl.pallas_call(
        functools.partial(_outproj_kernel, alpha=alpha),
        grid=(pl.cdiv(T, rb),),
        in_specs=[row(d), row(d), _const_spec(wout.shape), _const_spec((1, d)), _const_spec((1, d)),
                  _const_spec(wr_hi.shape), _const_spec(wr_lo.shape), _const_spec((1, LANES))],
        out_specs=[row(d), row(LANES)],
        out_shape=[jax.ShapeDtypeStruct((T, d), F32), jax.ShapeDtypeStruct((T, LANES), F32)],
        compiler_params=_params(("parallel",)),
        name="outproj_ln_router",
    )(m, h, wout, g.reshape(1, d), b.reshape(1, d), wr_hi, wr_lo, br)


def _lane_cumsum(x):
    lane = lax.broadcasted_iota(I32, x.shape, 1)
    s = 1
    while s < x.shape[1]:
        x = x + jnp.where(lane >= s, pltpu.roll(x, s, axis=1), 0.0)
        s *= 2
    return x


def _assign_onehots(route, i, tb, tokens):
    lane = lax.broadcasted_iota(I32, (tb, LANES), 1)
    row = i * tb + lax.broadcasted_iota(I32, (tb, 1), 0)
    live = row < tokens
    e0 = jnp.where(live, route[:, 0:1], -1.0).astype(I32)
    e1 = jnp.where(live, route[:, 1:2], -1.0).astype(I32)
    oh0 = jnp.where(lane == e0, 1.0, 0.0)
    oh1 = jnp.where(lane == e1, 1.0, 0.0)
    return oh0, oh1


def _count_kernel(route_ref, counts_ref, *, tokens, tb):
    i = pl.program_id(0)
    oh0, oh1 = _assign_onehots(route_ref[...], i, tb, tokens)

    @pl.when(i == 0)
    def _():
        counts_ref[...] = jnp.zeros_like(counts_ref)

    counts_ref[0:1, :] += jnp.sum(oh0 + oh1, axis=0, keepdims=True)


def _rank_kernel(route_ref, counts_ref, dest_ref, blk_ref, base, *, tokens, tb, blk, nblocks):
    i = pl.program_id(0)
    ne = N_GROUPS * EXPERTS_PER_GROUP
    lane = lax.broadcasted_iota(I32, (tb, LANES), 1)
    oh0, oh1 = _assign_onehots(route_ref[...], i, tb, tokens)
    both = oh0 + oh1

    @pl.when(i == 0)
    def _():
        lane1 = lax.broadcasted_iota(I32, (1, LANES), 1)
        cnt = jnp.where(lane1 < ne, counts_ref[0:1, :], 0.0)
        padded = jnp.ceil(cnt / blk) * blk
        pend = _lane_cumsum(padded)
        base[...] = pend - padded
        jrow = lax.broadcasted_iota(I32, (nblocks, LANES), 0).astype(F32) * blk
        lanej = lax.broadcasted_iota(I32, (nblocks, LANES), 1)
        ended = jnp.where((lanej < ne) & (pend <= jrow), 1.0, 0.0)
        bexp = jnp.minimum(jnp.sum(ended, axis=-1, keepdims=True), ne - 1.0)
        used = jnp.sum(jnp.where(lane1 == ne - 1, pend, 0.0), axis=-1, keepdims=True) / blk
        blk_ref[...] = jnp.where(lanej == 0, bexp, jnp.where(lanej == 1, used, 0.0))

    r_i = lax.broadcasted_iota(I32, (tb, tb), 0)
    c_i = lax.broadcasted_iota(I32, (tb, tb), 1)
    ltri = jnp.where(c_i < r_i, 1.0, 0.0).astype(BF16)
    prefix = jnp.dot(ltri, both.astype(BF16), preferred_element_type=F32)
    slot = base[...] + prefix
    d0 = jnp.sum(oh0 * slot, axis=-1, keepdims=True)
    d1 = jnp.sum(oh1 * slot, axis=-1, keepdims=True)
    dest_ref[...] = jnp.where(lane == 0, d0, jnp.where(lane == 1, d1, 0.0))
    base[...] += jnp.sum(both, axis=0, keepdims=True)


def _dispatch(route, blk):
    T = route.shape[0]
    tb = ROW_BLOCK
    ne = N_GROUPS * EXPERTS_PER_GROUP
    nblocks = (2 * T + ne * (blk - 1) + blk - 1) // blk
    nbp = _round_up(nblocks, 8)
    rspec = pl.BlockSpec((tb, LANES), lambda i: (i, 0))
    cspec = pl.BlockSpec((8, LANES), lambda i: (0, 0))
    counts = pl.pallas_call(
        functools.partial(_count_kernel, tokens=T, tb=tb),
        grid=(pl.cdiv(T, tb),),
        in_specs=[rspec],
        out_specs=cspec,
        out_shape=jax.ShapeDtypeStruct((8, LANES), F32),
        compiler_params=_params(("arbitrary",)),
        name="moe_counts",
    )(route)
    dest, blkinfo = pl.pallas_call(
        functools.partial(_rank_kernel, tokens=T, tb=tb, blk=blk, nblocks=nbp),
        grid=(pl.cdiv(T, tb),),
        in_specs=[rspec, cspec],
        out_specs=[rspec, pl.BlockSpec((nbp, LANES), lambda i: (0, 0))],
        out_shape=[jax.ShapeDtypeStruct((T, LANES), F32), jax.ShapeDtypeStruct((nbp, LANES), F32)],
        scratch_shapes=[pltpu.VMEM((1, LANES), F32)],
        compiler_params=_params(("arbitrary",)),
        name="moe_rank",
    )(route, counts)
    return dest, blkinfo, nblocks


def _scatter_kernel(dest_ref, x_ref, xs_in, xs_out, sem, *, tb):
    del xs_in

    def copy(r, s):
        d = dest_ref[0, 0, 2 * r + s]
        return d, pltpu.make_async_copy(x_ref.at[pl.ds(r, 1)], xs_out.at[pl.ds(jnp.maximum(d, 0), 1)], sem)

    def start(r, c):
        for s in range(2):
            d, cp = copy(r, s)

            @pl.when(d >= 0)
            def _():
                cp.start()
        return c

    def wait(r, c):
        for s in range(2):
            d, cp = copy(r, s)

            @pl.when(d >= 0)
            def _():
                cp.wait()
        return c

    lax.fori_loop(0, tb, start, 0)
    lax.fori_loop(0, tb, wait, 0)


def _moe_scatter(h1, dest_idx, rows, tb):
    T, d = h1.shape
    nblk = dest_idx.shape[0]
    xs0 = jnp.zeros((rows, d), F32)
    return pl.pallas_call(
        functools.partial(_scatter_kernel, tb=tb),
        grid=(nblk,),
        in_specs=[pl.BlockSpec((1, 1, 2 * tb), lambda i: (i, 0, 0), memory_space=pltpu.SMEM),
                  pl.BlockSpec((tb, d), lambda i: (i, 0)),
                  pl.BlockSpec(memory_space=pl.ANY)],
        out_specs=pl.BlockSpec(memory_space=pl.ANY),
        out_shape=jax.ShapeDtypeStruct((rows, d), F32),
        scratch_shapes=[pltpu.SemaphoreType.DMA(())],
        input_output_aliases={2: 0},
        compiler_params=_params(("arbitrary",)),
        name="moe_scatter",
    )(dest_idx, h1, xs0)


def _expert_kernel(be_ref, nu_ref, x_ref, wg_ref, wu_ref, wd_ref, y_ref):
    i = pl.program_id(0)

    @pl.when(i < nu_ref[0])
    def _():
        x = x_ref[...].astype(BF16)
        g = jnp.dot(x, wg_ref[0], preferred_element_type=F32)
        u = jnp.dot(x, wu_ref[0], preferred_element_type=F32)
        mid = (g * jax.nn.sigmoid(g) * u).astype(BF16)
        y_ref[...] = jnp.dot(mid, wd_ref[0], preferred_element_type=F32)

    @pl.when(i >= nu_ref[0])
    def _():
        y_ref[...] = jnp.zeros_like(y_ref)


def _moe_experts(xs, blk_e, n_used, wg, wu, wd, blk):
    rows, d = xs.shape
    de = wg.shape[-1]
    nblocks = rows // blk
    grid_spec = pltpu.PrefetchScalarGridSpec(
        num_scalar_prefetch=2,
        grid=(nblocks,),
        in_specs=[pl.BlockSpec((blk, d), lambda i, be, nu: (i, 0)),
                  pl.BlockSpec((1, d, de), lambda i, be, nu: (be[i], 0, 0)),
                  pl.BlockSpec((1, d, de), lambda i, be, nu: (be[i], 0, 0)),
                  pl.BlockSpec((1, de, d), lambda i, be, nu: (be[i], 0, 0))],
        out_specs=pl.BlockSpec((blk, d), lambda i, be, nu: (i, 0)),
    )
    return pl.pallas_call(
        _expert_kernel,
        grid_spec=grid_spec,
        out_shape=jax.ShapeDtypeStruct((rows, d), F32),
        compiler_params=_params(("arbitrary",)),
        name="moe_experts",
    )(blk_e, n_used, xs, wg, wu, wd)


def _combine_kernel(dest_ref, h_ref, route_ref, g_ref, b_ref, ys_hbm, o_ref, ob_ref, gbuf, sem, *, tb, alpha):
    def copy(r, s):
        d = dest_ref[0, 0, 2 * r + s]
        return pltpu.make_async_copy(ys_hbm.at[pl.ds(d, 1)], gbuf.at[s, pl.ds(r, 1)], sem)

    def start(r, c):
        copy(r, 0).start()
        copy(r, 1).start()
        return c

    def wait(r, c):
        copy(r, 0).wait()
        copy(r, 1).wait()
        return c

    lax.fori_loop(0, tb, start, 0)
    lax.fori_loop(0, tb, wait, 0)
    route = route_ref[...]
    moe = route[:, 2:3] * gbuf[0] + route[:, 3:4] * gbuf[1]
    y = _layer_norm_val(alpha * h_ref[...] + moe, g_ref[...], b_ref[...])
    o_ref[...] = y
    ob_ref[...] = y.astype(BF16)


def _moe_combine(h1, route, dest_idx, ys, g, b, alpha, tb):
    T, d = h1.shape
    nblk = dest_idx.shape[0]
    row = lambda w: pl.BlockSpec((tb, w), lambda i: (i, 0))
    return pl.pallas_call(
        functools.partial(_combine_kernel, tb=tb, alpha=alpha),
        grid=(nblk,),
        in_specs=[pl.BlockSpec((1, 1, 2 * tb), lambda i: (i, 0, 0), memory_space=pltpu.SMEM),
                  row(d), row(LANES), _const_spec((1, d)), _const_spec((1, d)),
                  pl.BlockSpec(memory_space=pl.ANY)],
        out_specs=[row(d), row(d)],
        out_shape=[jax.ShapeDtypeStruct((T, d), F32), jax.ShapeDtypeStruct((T, d), BF16)],
        scratch_shapes=[pltpu.VMEM((2, tb, d), F32), pltpu.SemaphoreType.DMA(())],
        compiler_params=_params(("arbitrary",)),
        name="moe_combine_ln",
    )(dest_idx, h1, route, g.reshape(1, d), b.reshape(1, d), ys)


def _moe(h1, route, wg, wu, wd, g, b, alpha):
    T, d = h1.shape
    blk = MOE_BLOCK
    tb = 256
    dest, blkinfo, nblocks = _dispatch(route, blk)
    nblk_tok = pl.cdiv(T, tb)
    di = dest[:, 0:2].astype(I32)
    pad = nblk_tok * tb - T
    scat_idx = jnp.pad(di, ((0, pad), (0, 0)), constant_values=-1).reshape(nblk_tok, 1, 2 * tb)
    gath_idx = jnp.pad(di, ((0, pad), (0, 0)), constant_values=0).reshape(nblk_tok, 1, 2 * tb)
    blk_e = blkinfo[:nblocks, 0].astype(I32)
    n_used = blkinfo[0:1, 1].astype(I32)
    xs = _moe_scatter(h1, scat_idx, nblocks * blk, tb)
    ys = _moe_experts(xs, blk_e, n_used, wg, wu, wd, blk)
    return _moe_combine(h1, route, gath_idx, ys, g, b, alpha, tb)


def _prep_weights(W):
    depth, d, _ = W["w_in"].shape
    dr = W["conv_a_w"].shape[-1]
    dh = W["conv_b_w"].shape[-1] // 3
    ql, kvl = W["q_norm_g"].shape[-1], W["kv_norm_g"].shape[-1]
    ca, cbw = 2 * dr, 3 * dh
    cc = ql + kvl + QK_ROPE
    w_in = W["w_in"]
    half = QK_ROPE // 2

    def rot(w):
        return jnp.concatenate([-w[..., half:], w[..., :half]], axis=-1)

    w_a = w_in[..., :ca]
    w_b = w_in[..., ca:ca + cbw]
    w_c = w_in[..., ca + cbw:ca + cbw + cc]
    w_g = w_in[..., ca + cbw + cc:]
    w_kr = w_c[..., ql + kvl:]
    zpad = jnp.zeros(w_kr.shape[:-1] + (LANES - QK_ROPE,), F32)
    w_c_ext = jnp.concatenate([w_c[..., :ql + kvl], w_kr, zpad, rot(w_kr), zpad], axis=-1)

    wq = W["w_uq"].reshape(depth, ql, N_HEADS, QK_NOPE + QK_ROPE)
    z_n = jnp.zeros((depth, ql, N_HEADS, QK_NOPE), F32)
    z_p = jnp.zeros((depth, ql, N_HEADS, HEAD_PAD - QK_NOPE - QK_ROPE), F32)
    wa = jnp.concatenate([wq, z_p], axis=-1).reshape(depth, ql, N_HEADS * HEAD_PAD)
    wb = jnp.concatenate([z_n, rot(wq[..., QK_NOPE:]), z_p], axis=-1).reshape(depth, ql, N_HEADS * HEAD_PAD)
    wkv = W["w_ukv"].reshape(depth, kvl, N_HEADS, QK_NOPE + V_DIM)
    wkn = wkv[..., :QK_NOPE].reshape(depth, kvl, N_HEADS * QK_NOPE)
    wv = wkv[..., QK_NOPE:].reshape(depth, kvl, N_HEADS * V_DIM)

    ng, ne = N_GROUPS, N_GROUPS * EXPERTS_PER_GROUP
    wr = jnp.concatenate([W["w_rg"], W["w_re"], jnp.zeros((depth, d, LANES - ng - ne), F32)], axis=-1)
    br = jnp.concatenate([W["b_rg"], W["b_re"], jnp.zeros((depth, LANES - ng - ne), F32)], axis=-1)
    wr_hi = wr.astype(BF16)
    wr_lo = (wr - wr_hi.astype(F32)).astype(BF16)
    bf = lambda x: x.astype(BF16)
    return dict(
        w_a=bf(w_a), w_b=bf(w_b), w_c=bf(w_c_ext), w_g=bf(w_g),
        conv_a_w=W["conv_a_w"], conv_a_b=W["conv_a_b"], rg_wa=bf(W["rg_wa"]), rg_ba=W["rg_ba"],
        rg_wx=bf(W["rg_wx"]), rg_bx=W["rg_bx"], rg_lambda=W["rg_lambda"],
        conv_b_w=W["conv_b_w"], conv_b_b=W["conv_b_b"], hy_w1=W["hy_w1"], hy_b1=W["hy_b1"],
        hy_w2=W["hy_w2"], hy_b2=W["hy_b2"], hy_w3=W["hy_w3"], hy_freq=W["hy_freq"], hy_skip=W["hy_skip"],
        q_norm_g=W["q_norm_g"], kv_norm_g=W["kv_norm_g"], wa=bf(wa), wb=bf(wb), wkn=bf(wkn), wv=bf(wv),
        w_pa=bf(W["w_pa"]), w_pb=bf(W["w_pb"]), w_pc=bf(W["w_pc"]), w_out=bf(W["w_out"]),
        ln1_g=W["ln1_g"], ln1_b=W["ln1_b"], wr_hi=wr_hi, wr_lo=wr_lo, br=br.reshape(depth, 1, LANES),
        w_e_gate=bf(W["w_e_gate"]), w_e_up=bf(W["w_e_up"]), w_e_down=bf(W["w_e_down"]),
        ln2_g=W["ln2_g"], ln2_b=W["ln2_b"],
    )


def _encoder_layer(h, hb, P, B, L, plan, tabs, rope_tab, alpha):
    T = B * L
    pa = _matmul(hb, P["w_a"], BF16, "in_proj_a")
    pb = _matmul(hb, P["w_b"], BF16, "in_proj_b")
    pc = _matmul(hb, P["w_c"], BF16, "in_proj_c")
    pg = _matmul(hb, P["w_g"], BF16, "in_proj_g")

    a, u = _rglru_gates(pa.reshape(B, L, -1), P["conv_a_w"], P["conv_a_b"], P["rg_wa"], P["rg_ba"],
                        P["rg_wx"], P["rg_bx"], P["rg_lambda"])
    h2 = _rglru_scan(a, u)

    s, x0 = _hy_conv3(pb.reshape(B, L, -1), P["conv_b_w"], P["conv_b_b"], plan)
    dh = s.shape[-1]
    hl, norm = _hy_taps(L, plan, P["hy_w1"], P["hy_b1"], P["hy_w2"], P["hy_b2"], P["hy_w3"], P["hy_freq"])
    f_r, f_i = _hy_filter_fft(hl, norm, plan, tabs)
    x_r, x_i = _hy_fwd_fft(s.reshape(B * plan.nb, plan.ns, dh), plan, tabs)
    y_r, y_i = _hy_mac(x_r, x_i, f_r, f_i, B, plan)
    y = _hy_inv_fft(y_r, y_i, plan, tabs).reshape(B, plan.nb * plan.ns, dh)
    if plan.nb * plan.ns != L and B > 1:
        y, s, x0 = y[:, :L], s[:, :L], x0[:, :L]
    flat = lambda t: t.reshape(-1, dh)

    q, k, v, kmax = _mla_prep(pc.reshape(B, L, -1), rope_tab, P["q_norm_g"], P["kv_norm_g"],
                        P["wa"], P["wb"], P["wkn"], P["wv"])
    o = _attention(q, k, v, kmax).reshape(T, -1)

    m = _merge(h2, pa, flat(y), flat(s), flat(x0), P["hy_skip"], o, pg, P["w_pa"], P["w_pb"], P["w_pc"])
    h1, route = _outproj(m, h, P["w_out"], P["ln1_g"], P["ln1_b"], P["wr_hi"], P["wr_lo"], P["br"], alpha)
    return _moe(h1, route, P["w_e_gate"], P["w_e_up"], P["w_e_down"], P["ln2_g"], P["ln2_b"], alpha)


def _encoder_trunk(x, meta, ln_g, ln_b, PW, alpha):
    B, S, D = x.shape
    L = S + N_META
    xc = jnp.concatenate([jnp.broadcast_to(meta[None], (B, N_META, D)), x], axis=1).reshape(B * L, D)
    h, hb = _layer_norm_rows(xc, ln_g, ln_b)
    plan = _FftPlan(L)
    tabs = plan.tables()
    rope_tab = _rope_tables(L)

    def body(carry, P):
        return _encoder_layer(carry[0], carry[1], P, B, L, plan, tabs, rope_tab, alpha), None

    (h, hb), _ = lax.scan(body, (h, hb), PW)
    return h.reshape(B, L, D)[:, N_META:]


def kernel(x_prompt, x_sample, meta, ln_emb_g, ln_emb_b, w_in, conv_a_w, conv_a_b, rg_wa, rg_ba, rg_wx, rg_bx, rg_lambda, conv_b_w, conv_b_b, hy_w1, hy_b1, hy_w2, hy_b2, hy_w3, hy_freq, hy_skip, q_norm_g, w_uq, kv_norm_g, w_ukv, w_pa, w_pb, w_pc, w_out, ln1_g, ln1_b, w_rg, b_rg, w_re, b_re, w_e_gate, w_e_up, w_e_down, ln2_g, ln2_b):
    W = dict(w_in=w_in, conv_a_w=conv_a_w, conv_a_b=conv_a_b, rg_wa=rg_wa, rg_ba=rg_ba, rg_wx=rg_wx,
             rg_bx=rg_bx, rg_lambda=rg_lambda, conv_b_w=conv_b_w, conv_b_b=conv_b_b, hy_w1=hy_w1,
             hy_b1=hy_b1, hy_w2=hy_w2, hy_b2=hy_b2, hy_w3=hy_w3, hy_freq=hy_freq, hy_skip=hy_skip,
             q_norm_g=q_norm_g, w_uq=w_uq, kv_norm_g=kv_norm_g, w_ukv=w_ukv, w_pa=w_pa, w_pb=w_pb,
             w_pc=w_pc, w_out=w_out, ln1_g=ln1_g, ln1_b=ln1_b, w_rg=w_rg, b_rg=b_rg, w_re=w_re, b_re=b_re,
             w_e_gate=w_e_gate, w_e_up=w_e_up, w_e_down=w_e_down, ln2_g=ln2_g, ln2_b=ln2_b)
    depth = w_in.shape[0]
    alpha = (2 * depth) ** 0.25
    PW = _prep_weights(W)
    y_prompt = _encoder_trunk(x_prompt, meta, ln_emb_g, ln_emb_b, PW, alpha)
    y_sample = _encoder_trunk(x_sample, meta, ln_emb_g, ln_emb_b, PW, alpha)
    return (y_prompt, y_sample)
```

```python
import functools
import math

import numpy as np
import jax
import jax.numpy as jnp
from jax import lax
from jax.experimental import pallas as pl
from jax.experimental.pallas import tpu as pltpu

F32 = jnp.float32
BF16 = jnp.bfloat16
I32 = jnp.int32

N_META = 16
RG_C = 8.0
HY_BANDS = 16
HY_FAST_DECAY = 0.3
HY_SLOW_DECAY = 1.5
HY_TARGET = 1e-2
N_HEADS = 16
QK_NOPE = 128
QK_ROPE = 64
V_DIM = 128
ROPE_THETA = 10000.0
N_GROUPS = 8
EXPERTS_PER_GROUP = 8
LN_EPS = 1e-5
RMS_EPS = 1e-6

V7X_VMEM_BYTES = 64 * 1024 * 1024
VMEM_LIMIT = 56 * 1024 * 1024
LANES = 128
HEAD_PAD = 2 * LANES

ROW_BLOCK = 512
MM_ROW_BLOCK = 1024
HALO = 16
HY_MAX_BLOCK = 4608
MOE_BLOCK = 256
MOE_TOKEN_BLOCK = 256
MERGE_ROW_BLOCK = 256
DMA_GROUP = 8
ATT_TQ = 384
ATT_CK = 512
ATT_UNROLL = 8
ATT_MIN_DENOM = 2.0 ** -60
DFT_UNROLL = 8
SCAN_BATCH_GROUP = 4
SCAN_BLOCK_BYTES = 9 * 512 * 1024


def _params(sem, vmem=VMEM_LIMIT):
    return pltpu.CompilerParams(dimension_semantics=sem, vmem_limit_bytes=vmem)


def _round_up(x, m):
    return (x + m - 1) // m * m


def _pick_tile(n, cands):
    for c in cands:
        if n % c == 0:
            return c
    return n


def _const_spec(shape):
    nd = len(shape)
    return pl.BlockSpec(shape, lambda *a: (0,) * nd, pipeline_mode=pl.Buffered(1))


def _layer_norm_val(x, g, b):
    mu = jnp.mean(x, axis=-1, keepdims=True)
    xc = x - mu
    var = jnp.mean(xc * xc, axis=-1, keepdims=True)
    return xc * lax.rsqrt(var + LN_EPS) * g + b


def _ln_kernel(x_ref, g_ref, b_ref, o_ref, ob_ref):
    y = _layer_norm_val(x_ref[...], g_ref[...], b_ref[...])
    o_ref[...] = y
    ob_ref[...] = y.astype(BF16)


def _layer_norm_rows(x, g, b):
    T, D = x.shape
    rb = ROW_BLOCK
    row = pl.BlockSpec((rb, D), lambda i: (i, 0))
    vec = pl.BlockSpec((1, D), lambda i: (0, 0))
    return pl.pallas_call(
        _ln_kernel,
        grid=(pl.cdiv(T, rb),),
        in_specs=[row, vec, vec],
        out_specs=[row, row],
        out_shape=[jax.ShapeDtypeStruct((T, D), F32), jax.ShapeDtypeStruct((T, D), BF16)],
        compiler_params=_params(("parallel",)),
        name="embed_ln",
    )(x, g.reshape(1, D), b.reshape(1, D))


def _mm_kernel(x_ref, w_ref, o_ref):
    o_ref[...] = jnp.dot(x_ref[...], w_ref[...], preferred_element_type=F32).astype(o_ref.dtype)


def _matmul(x, w, out_dtype, name):
    T, K = x.shape
    N = w.shape[1]
    rb = MM_ROW_BLOCK
    tn = _pick_tile(N, (1024, 768, 640, 512, 384, 256, 128))
    return pl.pallas_call(
        _mm_kernel,
        grid=(pl.cdiv(T, rb), N // tn),
        in_specs=[pl.BlockSpec((rb, K), lambda i, j: (i, 0)),
                  pl.BlockSpec((K, tn), lambda i, j: (0, j))],
        out_specs=pl.BlockSpec((rb, tn), lambda i, j: (i, j)),
        out_shape=jax.ShapeDtypeStruct((T, N), out_dtype),
        compiler_params=_params(("parallel", "arbitrary")),
        name=name,
    )(x, w)


def _halo_specs(tb, cb, n_in_blocks, n16, col_of):
    r = tb // HALO
    main = pl.BlockSpec((1, tb, cb), lambda b, i, c: (b, jnp.minimum(i, n_in_blocks - 1), col_of(c)))
    prev = pl.BlockSpec((1, HALO, cb), lambda b, i, c: (b, jnp.clip(i * r - 1, 0, n16 - 1), col_of(c)))
    nxt = pl.BlockSpec((1, HALO, cb), lambda b, i, c: (b, jnp.clip((i + 1) * r, 0, n16 - 1), col_of(c)))
    return [main, prev, nxt]


def _halo_window(xm, xp, xn, i, tb, seq_len):
    w = jnp.concatenate([xp[0], xm[0], xn[0]], axis=0).astype(F32)
    t = i * tb - HALO + lax.broadcasted_iota(I32, (tb + 2 * HALO, 1), 0)
    return jnp.where((t >= 0) & (t < seq_len), w, 0.0)


def _rglru_gate_kernel(xm, xp, xn, cw, cb, wa, ba, wx, bx, lam, a_ref, u_ref, *, tb, seq_len):
    i = pl.program_id(1)
    w = _halo_window(xm, xp, xn, i, tb, seq_len)
    o = HALO
    xb = (cw[0:1] * w[o - 1:o - 1 + tb] + cw[1:2] * w[o:o + tb]
          + cw[2:3] * w[o + 1:o + 1 + tb] + cw[3:4] * w[o + 2:o + 2 + tb] + cb[...])
    for k in range(xb.shape[1] // LANES):
        cols = slice(k * LANES, (k + 1) * LANES)
        xk = xb[:, cols]
        xkb = xk.astype(BF16)
        for d in range(2):
            r = jax.nn.sigmoid(jnp.dot(xkb, wa[d, k], preferred_element_type=F32) + ba[d:d + 1, cols])
            ig = jax.nn.sigmoid(jnp.dot(xkb, wx[d, k], preferred_element_type=F32) + bx[d:d + 1, cols])
            z = -lam[d:d + 1, cols]
            softplus = jnp.maximum(z, 0.0) + jnp.log(1.0 + jnp.exp(-jnp.abs(z)))
            log_a = -RG_C * r * softplus
            a_ref[d, 0, :, cols] = jnp.exp(log_a)
            u_ref[d, 0, :, cols] = jnp.sqrt(1.0 - jnp.exp(2.0 * log_a)) * (ig * xk)


def _rglru_gates(pa3, conv_w, conv_b, wa, ba, wx, bx, lam):
    B, L, _ = pa3.shape
    DR = conv_w.shape[1]
    tb = ROW_BLOCK
    nt = pl.cdiv(L, tb)
    n16 = L // HALO
    vec = lambda rows: pl.BlockSpec((rows, DR), lambda b, i, k: (0, 0))
    wspec = pl.BlockSpec(wa.shape, lambda b, i, k: (0, 0, 0, 0))
    out = pl.BlockSpec((2, 1, tb, DR), lambda b, i, k: (0, b, i, 0))
    return pl.pallas_call(
        functools.partial(_rglru_gate_kernel, tb=tb, seq_len=L),
        grid=(B, nt, 1),
        in_specs=_halo_specs(tb, DR, nt, n16, lambda k: 0)
        + [vec(4), vec(1), wspec, vec(2), wspec, vec(2), vec(2)],
        out_specs=[out, out],
        out_shape=[jax.ShapeDtypeStruct((2, B, L, DR), F32)] * 2,
        compiler_params=_params(("parallel", "parallel", "arbitrary")),
        name="rglru_gates",
    )(pa3, pa3, pa3, conv_w, conv_b.reshape(1, DR), wa, ba, wx, bx, lam)


def _scan_kernel(a_ref, u_ref, o_ref, h_ref, *, steps):
    d = pl.program_id(0)
    c = pl.program_id(2)

    @pl.when(c == 0)
    def _():
        h_ref[...] = jnp.zeros_like(h_ref)

    def body(i, h):
        t = i + d * (steps - 1 - 2 * i)
        h = a_ref[0, :, t] * h + u_ref[0, :, t]
        o_ref[0, :, t] = h
        return h

    h_ref[...] = lax.fori_loop(0, steps, body, h_ref[...], unroll=4)


def _rglru_scan(a, u):
    _, B, L, DR = a.shape
    r = DR // LANES
    a5 = a.reshape(2, B, L, r, LANES)
    u5 = u.reshape(2, B, L, r, LANES)
    bg = SCAN_BATCH_GROUP if B % SCAN_BATCH_GROUP == 0 else B
    budget = SCAN_BLOCK_BYTES // (bg * r * LANES * 4)
    steps = max(s for s in range(1, L + 1) if L % s == 0 and s <= max(budget, 1))
    nch = L // steps
    spec = pl.BlockSpec((1, bg, steps, r, LANES), lambda d, g, c: (d, g, c + d * (nch - 1 - 2 * c), 0, 0))
    h = pl.pallas_call(
        functools.partial(_scan_kernel, steps=steps),
        grid=(2, B // bg, nch),
        in_specs=[spec, spec],
        out_specs=spec,
        out_shape=jax.ShapeDtypeStruct((2, B, L, r, LANES), F32),
        scratch_shapes=[pltpu.VMEM((bg, r, LANES), F32)],
        compiler_params=_params(("arbitrary", "arbitrary", "arbitrary")),
        name="rglru_scan",
    )(a5, u5)
    return h.reshape(2, B * L, DR)


class _FftPlan:
    def __init__(self, seq_len):
        self.nb = -(-seq_len // HY_MAX_BLOCK)
        self.ns = _round_up(-(-seq_len // self.nb), 16)
        best = None
        for n1 in (30, 62, 94, 126):
            n2 = _round_up(-(-(2 * self.ns - 1) // n1), 16)
            cost = n1 * n2 * (n1 + n2)
            if best is None or cost < best[0]:
                best = (cost, n1, n2)
        _, self.n1, self.n2 = best
        self.n = self.n1 * self.n2
        self.k1 = self.n1 // 2 + 1
        self.j1 = _round_up(-(-self.ns // self.n2), 16)
        self.j1f = _round_up(self.n1, 16)
        self.rows = self.k1 * self.n2

    def tables(self):
        n1, n2, n, k1n = self.n1, self.n2, self.n, self.k1
        k1 = np.arange(k1n, dtype=np.float64)[None, :, None]
        j2 = np.arange(n2, dtype=np.float64)[:, None, None]

        def fwd(jcols):
            j1 = np.arange(jcols, dtype=np.float64)[None, None, :]
            ang = 2.0 * np.pi * k1 * (j1 * n2 + j2) / n
            live = (j1 < n1)
            return np.concatenate([np.cos(ang) * live, -np.sin(ang) * live], axis=1)

        g_sig = fwd(self.j1)
        g_filt = fwd(self.j1f)
        a = np.arange(n2, dtype=np.float64)
        phi = 2.0 * np.pi * np.outer(a, a) / n2
        c, s = np.cos(phi), np.sin(phi)
        m2f_r = np.concatenate([c, -s], axis=0)
        m2f_i = np.concatenate([s, c], axis=0)
        m2i_r = np.concatenate([c, s], axis=0)
        m2i_i = np.concatenate([-s, c], axis=0)
        j1 = np.arange(self.j1, dtype=np.float64)[None, :, None]
        kk = np.arange(k1n, dtype=np.float64)[None, None, :]
        ang = 2.0 * np.pi * kk * (j1 * n2 + j2) / n
        coef = np.where((kk == 0) | (kk == n1 // 2), 1.0, 2.0) / n
        gi_r = coef * np.cos(ang)
        gi_i = -coef * np.sin(ang)
        bf = lambda x: jnp.asarray(x, dtype=F32).astype(BF16)
        return dict(g_sig=bf(g_sig), g_filt=bf(g_filt), m2f_r=bf(m2f_r), m2f_i=bf(m2f_i),
                    m2i_r=bf(m2i_r), m2i_i=bf(m2i_i), gi_r=bf(gi_r), gi_i=bf(gi_i))


def _hy_conv3_kernel(*refs, tb, seq_len):
    x_refs = refs[0:9]
    cw = refs[9:12]
    cb = refs[12:15]
    s_ref, x0_ref = refs[15], refs[16]
    i = pl.program_id(1)
    o = HALO
    outs = []
    for p in range(3):
        w = _halo_window(x_refs[3 * p], x_refs[3 * p + 1], x_refs[3 * p + 2], i, tb, seq_len)
        outs.append(cw[p][0:1] * w[o - 1:o - 1 + tb] + cw[p][1:2] * w[o:o + tb]
                    + cw[p][2:3] * w[o + 1:o + 1 + tb] + cb[p][...])
    t = i * tb + lax.broadcasted_iota(I32, (tb, 1), 0)
    live = t < seq_len
    s_ref[0] = jnp.where(live, outs[1] * outs[2], 0.0)
    x0_ref[0] = jnp.where(live, outs[0], 0.0).astype(BF16)


def _hy_conv3(pb3, conv_w, conv_b, plan):
    B, L, _ = pb3.shape
    DH = conv_w.shape[1] // 3
    cb_ = _pick_tile(DH, (1024, 512, 256, 128))
    nc = DH // cb_
    tb = ROW_BLOCK
    lp = plan.nb * plan.ns
    nt_in = pl.cdiv(L, tb)
    n16 = L // HALO
    in_specs = []
    for p in range(3):
        in_specs += _halo_specs(tb, cb_, nt_in, n16, lambda c, p=p: p * nc + c)
    in_specs += [pl.BlockSpec((3, cb_), lambda b, i, c, p=p: (0, p * nc + c)) for p in range(3)]
    in_specs += [pl.BlockSpec((1, cb_), lambda b, i, c, p=p: (0, p * nc + c)) for p in range(3)]
    out = pl.BlockSpec((1, tb, cb_), lambda b, i, c: (b, i, c))
    return pl.pallas_call(
        functools.partial(_hy_conv3_kernel, tb=tb, seq_len=L),
        grid=(B, pl.cdiv(lp, tb), nc),
        in_specs=in_specs,
        out_specs=[out, out],
        out_shape=[jax.ShapeDtypeStruct((B, lp, DH), F32), jax.ShapeDtypeStruct((B, lp, DH), BF16)],
        compiler_params=_params(("parallel", "parallel", "parallel")),
        name="hyena_conv3",
    )(*([pb3] * 9), *([conv_w] * 3), *([conv_b.reshape(1, 3 * DH)] * 3))


def _hy_lag_table(seq_len, plan):
    half = plan.nb * plan.ns
    rows = _round_up(2 * half, ROW_BLOCK)
    lag = np.arange(rows) - half
    pos = np.abs(lag)
    valid = (np.arange(rows) < 2 * half) & (pos <= seq_len - 1)
    posc = np.minimum(pos, seq_len - 1)
    t01 = np.linspace(0.0, 1.0, seq_len, dtype=np.float32)
    ang = ((2.0 * math.pi / seq_len) * np.arange(seq_len, dtype=np.float32)).astype(np.float32)
    bands = np.linspace(1e-4, HY_BANDS - 1, HY_BANDS, dtype=np.float32)
    arg = (bands[None, :] * ang[:, None]).astype(np.float64)
    emb = np.concatenate([t01[:, None].astype(np.float64), np.cos(arg), -np.sin(arg)], axis=-1)
    table = np.zeros((rows, LANES), np.float32)
    ne = emb.shape[1]
    table[:, :ne] = emb[posc] * valid[:, None]
    table[:, ne] = valid
    table[:, ne + 1] = lag >= 0
    return jnp.asarray(table), ne


def _hy_taps_kernel(e_ref, w1, b1, w2, b2, w3, fr, rates, hl_ref, norm_ref, *, ne, dh):
    hi = lax.Precision.HIGHEST
    e = e_ref[...]
    f = fr[...]
    z = jnp.sin(f * (jnp.dot(e, w1[...], preferred_element_type=F32, precision=hi) + b1[...]))
    z = jnp.sin(f * (jnp.dot(z, w2[...], preferred_element_type=F32, precision=hi) + b2[...]))
    z3 = jnp.dot(z, w3[...], preferred_element_type=F32, precision=hi)
    t01 = e[:, 0:1]
    valid = e[:, ne:ne + 1]
    is_fwd = e[:, ne + 1:ne + 2]
    decay = jnp.exp(-t01 * rates[...])
    taps = jnp.where(is_fwd > 0.5, z3[:, :dh], z3[:, dh:]) * decay * valid
    hl_ref[...] = taps

    @pl.when(pl.program_id(0) == 0)
    def _():
        norm_ref[...] = jnp.zeros_like(norm_ref)

    norm_ref[0:1, :] += jnp.sum(jnp.abs(taps), axis=0, keepdims=True)


def _hy_taps(seq_len, plan, w1, b1, w2, b2, w3, freq):
    table, ne = _hy_lag_table(seq_len, plan)
    rows = table.shape[0]
    nf = w1.shape[1]
    dh = w3.shape[1] // 2
    w1p = jnp.zeros((LANES, nf), F32).at[:ne].set(w1)
    rates = np.abs(np.linspace(math.log(HY_TARGET) / HY_SLOW_DECAY, math.log(HY_TARGET) / HY_FAST_DECAY,
                               dh, dtype=np.float32)).reshape(1, dh)
    rb = ROW_BLOCK
    full = lambda a: pl.BlockSpec(a.shape, lambda i: (0,) * a.ndim)
    args = (table, w1p, b1.reshape(1, nf), w2, b2.reshape(1, nf), w3, freq.reshape(1, nf), jnp.asarray(rates))
    return pl.pallas_call(
        functools.partial(_hy_taps_kernel, ne=ne, dh=dh),
        grid=(rows // rb,),
        in_specs=[pl.BlockSpec((rb, LANES), lambda i: (i, 0))] + [full(a) for a in args[1:]],
        out_specs=[pl.BlockSpec((rb, dh), lambda i: (i, 0)), pl.BlockSpec((8, dh), lambda i: (0, 0))],
        out_shape=[jax.ShapeDtypeStruct((rows, dh), F32), jax.ShapeDtypeStruct((8, dh), F32)],
        compiler_params=_params(("arbitrary",)),
        name="hyena_taps",
    )(*args)


def _dft_forward(src_ref, g_ref, m2r_ref, m2i_ref, xr, xi, out_r, out_i, scale, plan, jcols):
    n2, k1n = plan.n2, plan.k1

    def stage1(j2, c):
        v = src_ref[pl.ds(j2, jcols, stride=n2), :].astype(BF16)
        p = jnp.dot(g_ref[j2], v, preferred_element_type=F32)
        st = pl.multiple_of(j2 * k1n, 16)
        xr[pl.ds(st, k1n), :] = p[:k1n]
        xi[pl.ds(st, k1n), :] = p[k1n:]
        return c

    lax.fori_loop(0, n2, stage1, 0, unroll=DFT_UNROLL)

    def stage2(k, c):
        st = pl.multiple_of(k * n2, 16)
        yr = xr[pl.ds(k, n2, stride=k1n), :].astype(BF16)
        yi = xi[pl.ds(k, n2, stride=k1n), :].astype(BF16)
        z = (jnp.dot(m2r_ref[...], yr, preferred_element_type=F32)
             + jnp.dot(m2i_ref[...], yi, preferred_element_type=F32))
        if scale is not None:
            z = z * scale
        out_r[0, pl.ds(st, n2), :] = z[:n2].astype(BF16)
        out_i[0, pl.ds(st, n2), :] = z[n2:].astype(BF16)
        return c

    lax.fori_loop(0, k1n, stage2, 0, unroll=DFT_UNROLL)


def _hy_filter_fft_kernel(hi_ref, lo_ref, norm_ref, g_ref, m2r_ref, m2i_ref, hr_ref, hi_out_ref,
                          fbuf, xr, xi, *, plan):
    ns, n = plan.ns, plan.n
    rows = fbuf.shape[0]
    fbuf[pl.ds(0, ns), :] = hi_ref[...]
    fbuf[pl.ds(ns, rows - ns), :] = jnp.zeros((rows - ns, fbuf.shape[1]), F32)
    fbuf[pl.ds(n - ns, ns), :] = lo_ref[...]
    fbuf[pl.ds(n - ns, 1), :] = jnp.zeros((1, fbuf.shape[1]), F32)
    inv_norm = 1.0 / norm_ref[0:1, :]
    _dft_forward(fbuf, g_ref, m2r_ref, m2i_ref, xr, xi, hr_ref, hi_out_ref, inv_norm, plan, plan.j1f)


def _hy_filter_fft(hl, norm, plan, tabs):
    dh = hl.shape[1]
    cb_ = LANES
    nbd = 2 * plan.nb - 1
    ns = plan.ns
    spec_out = pl.BlockSpec((1, plan.rows, cb_), lambda p, c: (p, 0, c))
    return pl.pallas_call(
        functools.partial(_hy_filter_fft_kernel, plan=plan),
        grid=(nbd, dh // cb_),
        in_specs=[pl.BlockSpec((ns, cb_), lambda p, c: (p + 1, c)),
                  pl.BlockSpec((ns, cb_), lambda p, c: (p, c)),
                  pl.BlockSpec((8, cb_), lambda p, c: (0, c)),
                  _const_spec(tabs["g_filt"].shape), _const_spec(tabs["m2f_r"].shape),
                  _const_spec(tabs["m2f_i"].shape)],
        out_specs=[spec_out, spec_out],
        out_shape=[jax.ShapeDtypeStruct((nbd, plan.rows, dh), BF16)] * 2,
        scratch_shapes=[pltpu.VMEM((plan.j1f * plan.n2, cb_), F32),
                        pltpu.VMEM((plan.rows, cb_), F32), pltpu.VMEM((plan.rows, cb_), F32)],
        compiler_params=_params(("parallel", "parallel")),
        name="hyena_filter_fft",
    )(hl, hl, norm, tabs["g_filt"], tabs["m2f_r"], tabs["m2f_i"])


def _hy_fwd_fft_kernel(s_ref, g_ref, m2r_ref, m2i_ref, xr_out, xi_out, sbuf, xr, xi, *, plan):
    ns = plan.ns
    rows = sbuf.shape[0]
    sbuf[pl.ds(0, ns), :] = s_ref[0]
    if rows > ns:
        sbuf[pl.ds(ns, rows - ns), :] = jnp.zeros((rows - ns, sbuf.shape[1]), F32)
    _dft_forward(sbuf, g_ref, m2r_ref, m2i_ref, xr, xi, xr_out, xi_out, None, plan, plan.j1)


def _hy_fwd_fft(s_blocks, plan, tabs):
    q, ns, dh = s_blocks.shape
    cb_ = LANES
    spec_out = pl.BlockSpec((1, plan.rows, cb_), lambda b, c: (b, 0, c))
    return pl.pallas_call(
        functools.partial(_hy_fwd_fft_kernel, plan=plan),
        grid=(q, dh // cb_),
        in_specs=[pl.BlockSpec((1, ns, cb_), lambda b, c: (b, 0, c)),
                  _const_spec(tabs["g_sig"].shape), _const_spec(tabs["m2f_r"].shape),
                  _const_spec(tabs["m2f_i"].shape)],
        out_specs=[spec_out, spec_out],
        out_shape=[jax.ShapeDtypeStruct((q, plan.rows, dh), BF16)] * 2,
        scratch_shapes=[pltpu.VMEM((plan.j1 * plan.n2, cb_), F32),
                        pltpu.VMEM((plan.rows, cb_), F32), pltpu.VMEM((plan.rows, cb_), F32)],
        compiler_params=_params(("parallel", "parallel")),
        name="hyena_fwd_fft",
    )(s_blocks, tabs["g_sig"], tabs["m2f_r"], tabs["m2f_i"])


def _hy_mac_kernel(xr_ref, xi_ref, hr_ref, hi_ref, yr_ref, yi_ref, *, nb):
    for a in range(nb):
        yr = None
        for b in range(nb):
            d = a - b + nb - 1
            xr, xi = xr_ref[0, b].astype(F32), xi_ref[0, b].astype(F32)
            hr, hi = hr_ref[d].astype(F32), hi_ref[d].astype(F32)
            pr = xr * hr - xi * hi
            pi = xr * hi + xi * hr
            yr, yi = (pr, pi) if yr is None else (yr + pr, yi + pi)
        yr_ref[0, a] = yr.astype(BF16)
        yi_ref[0, a] = yi.astype(BF16)


def _hy_mac(xr, xi, hr, hi, batch, plan):
    nb = plan.nb
    dh = xr.shape[-1]
    rows = plan.rows
    rc = _pick_tile(rows, (256, 128, 64, 32, 16))
    cc = _pick_tile(dh, (1024, 512, 256, 128))
    x4 = lambda a: a.reshape(batch, nb, rows, dh)
    xs = pl.BlockSpec((1, nb, rc, cc), lambda b, r, c: (b, 0, r, c))
    hs = pl.BlockSpec((2 * nb - 1, rc, cc), lambda b, r, c: (0, r, c))
    yr, yi = pl.pallas_call(
        functools.partial(_hy_mac_kernel, nb=nb),
        grid=(batch, rows // rc, dh // cc),
        in_specs=[xs, xs, hs, hs],
        out_specs=[xs, xs],
        out_shape=[jax.ShapeDtypeStruct((batch, nb, rows, dh), BF16)] * 2,
        compiler_params=_params(("parallel", "parallel", "parallel")),
        name="hyena_freq_mac",
    )(x4(xr), x4(xi), hr, hi)
    return yr.reshape(batch * nb, rows, dh), yi.reshape(batch * nb, rows, dh)


def _hy_inv_fft_kernel(yr_ref, yi_ref, m2r_ref, m2i_ref, gr_ref, gi_ref, y_ref, ur, ui, ybuf, *, plan):
    n2, k1n, j1 = plan.n2, plan.k1, plan.j1

    def stage2(k, c):
        st = pl.multiple_of(k * n2, 16)
        w = (jnp.dot(m2r_ref[...], yr_ref[0, pl.ds(st, n2), :], preferred_element_type=F32)
             + jnp.dot(m2i_ref[...], yi_ref[0, pl.ds(st, n2), :], preferred_element_type=F32))
        ur[pl.ds(st, n2), :] = w[:n2]
        ui[pl.ds(st, n2), :] = w[n2:]
        return c

    lax.fori_loop(0, k1n, stage2, 0, unroll=DFT_UNROLL)

    def stage1(j2, c):
        vr = ur[pl.ds(j2, k1n, stride=n2), :].astype(BF16)
        vi = ui[pl.ds(j2, k1n, stride=n2), :].astype(BF16)
        y = (jnp.dot(gr_ref[j2], vr, preferred_element_type=F32)
             + jnp.dot(gi_ref[j2], vi, preferred_element_type=F32))
        ybuf[pl.ds(j2, j1, stride=n2), :] = y
        return c

    lax.fori_loop(0, n2, stage1, 0, unroll=DFT_UNROLL)
    y_ref[0] = ybuf[pl.ds(0, plan.ns), :]


def _hy_inv_fft(yr, yi, plan, tabs):
    q, rows, dh = yr.shape
    cb_ = LANES
    spec_in = pl.BlockSpec((1, rows, cb_), lambda b, c: (b, 0, c))
    return pl.pallas_call(
        functools.partial(_hy_inv_fft_kernel, plan=plan),
        grid=(q, dh // cb_),
        in_specs=[spec_in, spec_in,
                  _const_spec(tabs["m2i_r"].shape), _const_spec(tabs["m2i_i"].shape),
                  _const_spec(tabs["gi_r"].shape), _const_spec(tabs["gi_i"].shape)],
        out_specs=pl.BlockSpec((1, plan.ns, cb_), lambda b, c: (b, 0, c)),
        out_shape=jax.ShapeDtypeStruct((q, plan.ns, dh), F32),
        scratch_shapes=[pltpu.VMEM((rows, cb_), F32), pltpu.VMEM((rows, cb_), F32),
                        pltpu.VMEM((plan.j1 * plan.n2, cb_), F32)],
        compiler_params=_params(("parallel", "parallel")),
        name="hyena_inv_fft",
    )(yr, yi, tabs["m2i_r"], tabs["m2i_i"], tabs["gi_r"], tabs["gi_i"])


def _rope_tables(seq_len):
    half = QK_ROPE // 2
    inv = 1.0 / (ROPE_THETA ** (jnp.arange(0, QK_ROPE, 2, dtype=F32) / QK_ROPE))
    ang = jnp.arange(seq_len, dtype=F32)[:, None] * inv[None, :]
    cos2 = jnp.concatenate([jnp.cos(ang)] * 2, axis=-1)
    sin2 = jnp.concatenate([jnp.sin(ang)] * 2, axis=-1)
    scale = (QK_NOPE + QK_ROPE) ** -0.5 * math.log2(math.e)
    zpad = jnp.zeros((seq_len, HEAD_PAD - QK_NOPE - QK_ROPE), F32)
    ones = jnp.ones((seq_len, QK_NOPE), F32)
    zer = jnp.zeros((seq_len, QK_NOPE), F32)
    zk = jnp.zeros((seq_len, LANES - QK_ROPE), F32)
    del half
    return jnp.concatenate([scale * ones, scale * cos2, zpad, zer, scale * sin2, zpad,
                            cos2, zk, sin2, zk], axis=-1)


def _mla_prep_kernel(pc_ref, tab_ref, gq_ref, gkv_ref, wa_ref, wb_ref, wkn_ref, wv_ref,
                     q_ref, k_ref, v_ref, kmax_ref, *, ql, kvl, seq_len, tb):
    hp = HEAD_PAD
    i = pl.program_id(1)
    live = (i * tb + lax.broadcasted_iota(I32, (tb, 1), 0)) < seq_len
    lane8 = lax.broadcasted_iota(I32, (8, LANES), 1)

    @pl.when(i == 0)
    def _():
        kmax_ref[...] = jnp.zeros_like(kmax_ref)

    pc = pc_ref[0].astype(F32)
    tab = tab_ref[...]
    qc, qs = tab[:, 0:hp], tab[:, hp:2 * hp]
    kc, ks = tab[:, 2 * hp:2 * hp + LANES], tab[:, 2 * hp + LANES:2 * hp + 2 * LANES]

    def rms(x, g):
        return (x * lax.rsqrt(jnp.mean(x * x, axis=-1, keepdims=True) + RMS_EPS) * g).astype(BF16)

    cq = rms(pc[:, 0:ql], gq_ref[...])
    ckv = rms(pc[:, ql:ql + kvl], gkv_ref[...])
    kr = pc[:, ql + kvl:ql + kvl + LANES] * kc + pc[:, ql + kvl + LANES:ql + kvl + 2 * LANES] * ks
    kr = kr.astype(BF16)
    kr_sq = jnp.sum(jnp.square(kr.astype(F32)), axis=-1, keepdims=True)
    lane_t = lax.broadcasted_iota(I32, (tb, hp - V_DIM), 1)
    ones_col = jnp.where(lane_t == 0, 1.0, 0.0).astype(BF16)
    k_tail = jnp.where(lane_t == QK_ROPE, 1.0, kr.astype(F32)).astype(BF16)
    kmax = kmax_ref[0]
    for h in range(N_HEADS):
        qa = jnp.dot(cq, wa_ref[:, h * hp:(h + 1) * hp], preferred_element_type=F32)
        qb = jnp.dot(cq, wb_ref[:, h * hp:(h + 1) * hp], preferred_element_type=F32)
        q_ref[0, :, h * hp:(h + 1) * hp] = (qa * qc + qb * qs).astype(BF16)
        kn = jnp.dot(ckv, wkn_ref[:, h * QK_NOPE:(h + 1) * QK_NOPE], preferred_element_type=F32).astype(BF16)
        k_ref[0, :, h * hp:h * hp + QK_NOPE] = kn
        k_ref[0, :, h * hp + QK_NOPE:(h + 1) * hp] = k_tail
        k_sq = jnp.sum(jnp.square(kn.astype(F32)), axis=-1, keepdims=True) + kr_sq
        k_sq_max = jnp.max(jnp.where(live, k_sq, 0.0), axis=0, keepdims=True)
        kmax = jnp.where(lane8 == h, jnp.maximum(kmax, k_sq_max), kmax)
        vv = jnp.dot(ckv, wv_ref[:, h * V_DIM:(h + 1) * V_DIM], preferred_element_type=F32)
        v_ref[0, :, h * hp:h * hp + V_DIM] = vv.astype(BF16)
        v_ref[0, :, h * hp + V_DIM:(h + 1) * hp] = ones_col
    kmax_ref[0] = kmax


def _mla_prep(pc3, tab, gq, gkv, wa, wb, wkn, wv):
    B, L, pcw = pc3.shape
    ql, kvl = gq.shape[-1], gkv.shape[-1]
    tb = ROW_BLOCK
    hq = N_HEADS * HEAD_PAD
    hv = N_HEADS * HEAD_PAD
    blk = lambda w: pl.BlockSpec((1, tb, w), lambda b, i: (b, i, 0))
    return pl.pallas_call(
        functools.partial(_mla_prep_kernel, ql=ql, kvl=kvl, seq_len=L, tb=tb),
        grid=(B, pl.cdiv(L, tb)),
        in_specs=[blk(pcw), pl.BlockSpec((tb, tab.shape[1]), lambda b, i: (i, 0)),
                  _const_spec((1, ql)), _const_spec((1, kvl)),
                  _const_spec(wa.shape), _const_spec(wb.shape), _const_spec(wkn.shape), _const_spec(wv.shape)],
        out_specs=[blk(hq), blk(hq), blk(hv), pl.BlockSpec((1, 8, LANES), lambda b, i: (b, 0, 0))],
        out_shape=[jax.ShapeDtypeStruct((B, L, hq), BF16), jax.ShapeDtypeStruct((B, L, hq), BF16),
                   jax.ShapeDtypeStruct((B, L, hv), BF16), jax.ShapeDtypeStruct((B, 8, LANES), F32)],
        compiler_params=_params(("parallel", "arbitrary")),
        name="mla_prep",
    )(pc3, tab, gq.reshape(1, ql), gkv.reshape(1, kvl), wa, wb, wkn, wv)


def _attn_kernel(q_ref, k_ref, v_ref, kmax_ref, o_ref, *, seq_len, ck, unroll):
    h = pl.program_id(1)
    i = pl.program_id(2)
    q = q_ref[0]
    tq = q.shape[0]
    nt = (((1,), (1,)), ((), ()))
    full = seq_len // ck
    lead = seq_len % ck

    def over_chunks(step, state):
        if lead:
            state = step(0, lead, state)
        for c in range(full % unroll):
            state = step(lead + c * ck, ck, state)
        base = lead + (full % unroll) * ck

        def body(it, st):
            for c in range(unroll):
                st = step(pl.multiple_of(base + (it * unroll + c) * ck, 16), ck, st)
            return st

        return lax.fori_loop(0, full // unroll, body, state)

    qf = q.astype(F32)
    q_norm = jnp.sqrt(jnp.sum(qf * qf, axis=-1, keepdims=True))
    lane8 = lax.broadcasted_iota(I32, (8, LANES), 1)
    k_sq_max = jnp.max(jnp.where(lane8 == h, kmax_ref[0], 0.0), axis=(0, 1), keepdims=True)
    bound = q_norm * jnp.sqrt(k_sq_max)
    lane = lax.broadcasted_iota(I32, (tq, HEAD_PAD), 1)
    q_shift = jnp.where(lane == QK_NOPE + QK_ROPE, (-bound).astype(BF16), q)

    def fast(start, size, acc):
        s = lax.dot_general(q_shift, k_ref[0, pl.ds(start, size), :], nt, preferred_element_type=F32)
        p = jnp.exp2(s).astype(BF16)
        return acc + jnp.dot(p, v_ref[0, pl.ds(start, size), :], preferred_element_type=F32)

    acc = over_chunks(fast, jnp.zeros((tq, HEAD_PAD), F32))
    denom = acc[:, V_DIM:V_DIM + 1]
    o_ref[0] = (acc[:, :V_DIM] / denom).astype(BF16)

    rows = i * tq + lax.broadcasted_iota(I32, (tq, 1), 0)
    healthy = jnp.where(rows < seq_len, denom, 1.0) >= ATT_MIN_DENOM
    n_bad = jnp.sum(jnp.where(healthy, 0.0, 1.0))

    @pl.when(n_bad > 0.0)
    def _():
        def exact(start, size, state):
            m, acc_e = state
            s = lax.dot_general(q, k_ref[0, pl.ds(start, size), :], nt, preferred_element_type=F32)
            m_new = jnp.maximum(m, jnp.max(s, axis=-1, keepdims=True))
            p = jnp.exp2(s - m_new).astype(BF16)
            acc_e = jnp.exp2(m - m_new) * acc_e + jnp.dot(
                p, v_ref[0, pl.ds(start, size), :], preferred_element_type=F32)
            return m_new, acc_e

        _, acc_e = over_chunks(exact, (jnp.full((tq, 1), -1e30, F32), jnp.zeros((tq, HEAD_PAD), F32)))
        o_ref[0] = (acc_e[:, :V_DIM] / acc_e[:, V_DIM:V_DIM + 1]).astype(BF16)


def _attention(q, k, v, kmax):
    B, L, _ = q.shape
    tq = ATT_TQ
    ck = min(ATT_CK, L)
    return pl.pallas_call(
        functools.partial(_attn_kernel, seq_len=L, ck=ck, unroll=ATT_UNROLL),
        grid=(B, N_HEADS, pl.cdiv(L, tq)),
        in_specs=[pl.BlockSpec((1, tq, HEAD_PAD), lambda b, h, i: (b, i, h)),
                  pl.BlockSpec((1, L, HEAD_PAD), lambda b, h, i: (b, 0, h)),
                  pl.BlockSpec((1, L, HEAD_PAD), lambda b, h, i: (b, 0, h)),
                  pl.BlockSpec((1, 8, LANES), lambda b, h, i: (b, 0, 0))],
        out_specs=pl.BlockSpec((1, tq, V_DIM), lambda b, h, i: (b, i, h)),
        out_shape=jax.ShapeDtypeStruct((B, L, N_HEADS * V_DIM), BF16),
        compiler_params=_params(("parallel", "parallel", "arbitrary")),
        name="mla_attention",
    )(q, k, v, kmax)


def _merge_kernel(h2_ref, gb_ref, y_ref, s_ref, x0_ref, skip_ref, o_ref, pg_ref,
                  wpa_ref, wpb_ref, wpc_ref, m_ref, *, d):
    gb = gb_ref[...].astype(F32)
    xa = (jax.nn.gelu(gb) * (h2_ref[0] + h2_ref[1])).astype(BF16)
    xb = (x0_ref[...].astype(F32) * (y_ref[...] + s_ref[...] * skip_ref[...])).astype(BF16)
    ya = jnp.dot(xa, wpa_ref[...], preferred_element_type=F32)
    yb = jnp.dot(xb, wpb_ref[...], preferred_element_type=F32)
    yc = jnp.dot(o_ref[...], wpc_ref[...], preferred_element_type=F32)
    g = jax.nn.sigmoid(pg_ref[...].astype(F32))
    m_ref[...] = (g[:, 0:d] * ya + g[:, d:2 * d] * yb + g[:, 2 * d:3 * d] * yc).astype(BF16)


def _merge(h2, pa, y, s, x0, skip, o, pg, wpa, wpb, wpc):
    T = o.shape[0]
    d = wpa.shape[1]
    dr, dh = wpa.shape[0], wpb.shape[0]
    rb = MERGE_ROW_BLOCK
    row = lambda w, col=0: pl.BlockSpec((rb, w), lambda i: (i, col))
    return pl.pallas_call(
        functools.partial(_merge_kernel, d=d),
        grid=(pl.cdiv(T, rb),),
        in_specs=[pl.BlockSpec((2, rb, dr), lambda i: (0, i, 0)), row(dr, 1), row(dh), row(dh), row(dh),
                  _const_spec((1, dh)), row(o.shape[1]), row(3 * d),
                  _const_spec(wpa.shape), _const_spec(wpb.shape), _const_spec(wpc.shape)],
        out_specs=row(d),
        out_shape=jax.ShapeDtypeStruct((T, d), BF16),
        compiler_params=_params(("parallel",)),
        name="branch_merge",
    )(h2, pa, y, s, x0, skip.reshape(1, dh), o, pg, wpa, wpb, wpc)


def _route(logits):
    ng, ne = N_GROUPS, N_GROUPS * EXPERTS_PER_GROUP
    lane = lax.broadcasted_iota(I32, logits.shape, 1)
    neg = jnp.float32(-jnp.inf)
    big = jnp.int32(4 * LANES)
    gl = jnp.where(lane < ng, logits, neg)
    gmax = jnp.max(gl, axis=-1, keepdims=True)
    g_sel = jnp.min(jnp.where(gl == gmax, lane, big), axis=-1, keepdims=True)
    p_sel = 1.0 / jnp.sum(jnp.exp(gl - gmax), axis=-1, keepdims=True)
    e_lane = lane - ng
    grp_shift = EXPERTS_PER_GROUP.bit_length() - 1
    in_grp = (e_lane >= 0) & (e_lane < ne) & (lax.shift_right_arithmetic(e_lane, grp_shift) == g_sel)
    el = jnp.where(in_grp, logits, neg)
    v0 = jnp.max(el, axis=-1, keepdims=True)
    i0 = jnp.min(jnp.where(el == v0, lane, big), axis=-1, keepdims=True)
    el1 = jnp.where(lane == i0, neg, el)
    v1 = jnp.max(el1, axis=-1, keepdims=True)
    i1 = jnp.min(jnp.where(el1 == v1, lane, big), axis=-1, keepdims=True)
    t = jnp.exp(v1 - v0)
    w0 = p_sel / (1.0 + t)
    w1 = p_sel * t / (1.0 + t)
    e0 = (i0 - ng).astype(F32)
    e1 = (i1 - ng).astype(F32)
    return jnp.where(lane == 0, e0, jnp.where(lane == 1, e1, jnp.where(lane == 2, w0, jnp.where(lane == 3, w1, 0.0))))


def _outproj_kernel(m_ref, h_ref, wout_ref, g_ref, b_ref, wr_hi_ref, wr_lo_ref, br_ref,
                    h1_ref, route_ref, *, alpha):
    mixed = jnp.dot(m_ref[...], wout_ref[...], preferred_element_type=F32)
    h1 = _layer_norm_val(alpha * h_ref[...] + mixed, g_ref[...], b_ref[...])
    h1_ref[...] = h1
    hi = h1.astype(BF16)
    lo = (h1 - hi.astype(F32)).astype(BF16)
    logits = (jnp.dot(hi, wr_hi_ref[...], preferred_element_type=F32)
              + jnp.dot(lo, wr_hi_ref[...], preferred_element_type=F32)
              + jnp.dot(hi, wr_lo_ref[...], preferred_element_type=F32)) + br_ref[...]
    route_ref[...] = _route(logits)


def _outproj(m, h, wout, g, b, wr_hi, wr_lo, br, alpha):
    T, d = h.shape
    rb = ROW_BLOCK
    row = lambda w: pl.BlockSpec((rb, w), lambda i: (i, 0))
    return pl.pallas_call(
        functools.partial(_outproj_kernel, alpha=alpha),
        grid=(pl.cdiv(T, rb),),
        in_specs=[row(d), row(d), _const_spec(wout.shape), _const_spec((1, d)), _const_spec((1, d)),
                  _const_spec(wr_hi.shape), _const_spec(wr_lo.shape), _const_spec((1, LANES))],
        out_specs=[row(d), row(LANES)],
        out_shape=[jax.ShapeDtypeStruct((T, d), F32), jax.ShapeDtypeStruct((T, LANES), F32)],
        compiler_params=_params(("parallel",)),
        name="outproj_ln_router",
    )(m, h, wout, g.reshape(1, d), b.reshape(1, d), wr_hi, wr_lo, br)


def _lane_cumsum(x):
    lane = lax.broadcasted_iota(I32, x.shape, 1)
    s = 1
    while s < x.shape[1]:
        x = x + jnp.where(lane >= s, pltpu.roll(x, s, axis=1), 0.0)
        s *= 2
    return x


def _assign_onehots(route, i, tb, tokens):
    lane = lax.broadcasted_iota(I32, (tb, LANES), 1)
    row = i * tb + lax.broadcasted_iota(I32, (tb, 1), 0)
    live = row < tokens
    e0 = jnp.where(live, route[:, 0:1], -1.0).astype(I32)
    e1 = jnp.where(live, route[:, 1:2], -1.0).astype(I32)
    oh0 = jnp.where(lane == e0, 1.0, 0.0)
    oh1 = jnp.where(lane == e1, 1.0, 0.0)
    return oh0, oh1


def _count_kernel(route_ref, counts_ref, *, tokens, tb):
    i = pl.program_id(0)
    oh0, oh1 = _assign_onehots(route_ref[...], i, tb, tokens)

    @pl.when(i == 0)
    def _():
        counts_ref[...] = jnp.zeros_like(counts_ref)

    counts_ref[0:1, :] += jnp.sum(oh0 + oh1, axis=0, keepdims=True)


def _rank_kernel(route_ref, counts_ref, dest_ref, blk_ref, base, *, tokens, tb, blk, nblocks):
    i = pl.program_id(0)
    ne = N_GROUPS * EXPERTS_PER_GROUP
    lane = lax.broadcasted_iota(I32, (tb, LANES), 1)
    oh0, oh1 = _assign_onehots(route_ref[...], i, tb, tokens)
    both = oh0 + oh1

    @pl.when(i == 0)
    def _():
        lane1 = lax.broadcasted_iota(I32, (1, LANES), 1)
        cnt = jnp.where(lane1 < ne, counts_ref[0:1, :], 0.0)
        padded = jnp.ceil(cnt / blk) * blk
        pend = _lane_cumsum(padded)
        base[...] = pend - padded
        jrow = lax.broadcasted_iota(I32, (nblocks, LANES), 0).astype(F32) * blk
        lanej = lax.broadcasted_iota(I32, (nblocks, LANES), 1)
        ended = jnp.where((lanej < ne) & (pend <= jrow), 1.0, 0.0)
        bexp = jnp.minimum(jnp.sum(ended, axis=-1, keepdims=True), ne - 1.0)
        used = jnp.sum(jnp.where(lane1 == ne - 1, pend, 0.0), axis=-1, keepdims=True) / blk
        blk_ref[...] = jnp.where(lanej == 0, bexp, jnp.where(lanej == 1, used, 0.0))

    r_i = lax.broadcasted_iota(I32, (tb, tb), 0)
    c_i = lax.broadcasted_iota(I32, (tb, tb), 1)
    ltri = jnp.where(c_i < r_i, 1.0, 0.0).astype(BF16)
    prefix = jnp.dot(ltri, both.astype(BF16), preferred_element_type=F32)
    slot = base[...] + prefix
    d0 = jnp.sum(oh0 * slot, axis=-1, keepdims=True)
    d1 = jnp.sum(oh1 * slot, axis=-1, keepdims=True)
    dest_ref[...] = jnp.where(lane == 0, d0, jnp.where(lane == 1, d1, 0.0))
    base[...] += jnp.sum(both, axis=0, keepdims=True)


def _dispatch(route, blk):
    T = route.shape[0]
    tb = ROW_BLOCK
    ne = N_GROUPS * EXPERTS_PER_GROUP
    nblocks = (2 * T + ne * (blk - 1) + blk - 1) // blk
    nbp = _round_up(nblocks, 8)
    rspec = pl.BlockSpec((tb, LANES), lambda i: (i, 0))
    cspec = pl.BlockSpec((8, LANES), lambda i: (0, 0))
    counts = pl.pallas_call(
        functools.partial(_count_kernel, tokens=T, tb=tb),
        grid=(pl.cdiv(T, tb),),
        in_specs=[rspec],
        out_specs=cspec,
        out_shape=jax.ShapeDtypeStruct((8, LANES), F32),
        compiler_params=_params(("arbitrary",)),
        name="moe_counts",
    )(route)
    dest, blkinfo = pl.pallas_call(
        functools.partial(_rank_kernel, tokens=T, tb=tb, blk=blk, nblocks=nbp),
        grid=(pl.cdiv(T, tb),),
        in_specs=[rspec, cspec],
        out_specs=[rspec, pl.BlockSpec((nbp, LANES), lambda i: (0, 0))],
        out_shape=[jax.ShapeDtypeStruct((T, LANES), F32), jax.ShapeDtypeStruct((nbp, LANES), F32)],
        scratch_shapes=[pltpu.VMEM((1, LANES), F32)],
        compiler_params=_params(("arbitrary",)),
        name="moe_rank",
    )(route, counts)
    return dest, blkinfo, nblocks


def _for_row_groups(n_rows, fn):
    def body(g, c):
        for j in range(DMA_GROUP):
            fn(g * DMA_GROUP + j)
        return c

    lax.fori_loop(0, n_rows // DMA_GROUP, body, 0)


def _scatter_kernel(dest_ref, x_ref, xs_in, xs_out, sem, *, tb, tokens):
    del xs_in
    live_rows = jnp.minimum(tb, tokens - pl.program_id(0) * tb)

    def copy(r, s):
        d = dest_ref[0, 0, 2 * r + s]
        return pltpu.make_async_copy(x_ref.at[pl.ds(r, 1)], xs_out.at[pl.ds(d, 1)], sem)

    def start(r):
        copy(r, 0).start()
        copy(r, 1).start()

    def wait(r):
        copy(r, 0).wait()
        copy(r, 1).wait()

    _for_row_groups(live_rows, start)
    _for_row_groups(live_rows, wait)


def _moe_scatter(h1, dest_idx, rows, tb):
    T, d = h1.shape
    assert T % DMA_GROUP == 0
    nblk = dest_idx.shape[0]
    xs0 = jnp.zeros((rows, d), F32)
    return pl.pallas_call(
        functools.partial(_scatter_kernel, tb=tb, tokens=T),
        grid=(nblk,),
        in_specs=[pl.BlockSpec((1, 1, 2 * tb), lambda i: (i, 0, 0), memory_space=pltpu.SMEM),
                  pl.BlockSpec((tb, d), lambda i: (i, 0)),
                  pl.BlockSpec(memory_space=pl.ANY)],
        out_specs=pl.BlockSpec(memory_space=pl.ANY),
        out_shape=jax.ShapeDtypeStruct((rows, d), F32),
        scratch_shapes=[pltpu.SemaphoreType.DMA(())],
        input_output_aliases={2: 0},
        compiler_params=_params(("arbitrary",)),
        name="moe_scatter",
    )(dest_idx, h1, xs0)


def _expert_kernel(be_ref, nu_ref, ly_ref, x_ref, wg_ref, wu_ref, wd_ref, y_ref):
    del be_ref, ly_ref
    i = pl.program_id(0)

    @pl.when(i < nu_ref[0])
    def _():
        x = x_ref[...].astype(BF16)
        g = jnp.dot(x, wg_ref[0, 0], preferred_element_type=F32)
        u = jnp.dot(x, wu_ref[0, 0], preferred_element_type=F32)
        mid = (g * jax.nn.sigmoid(g) * u).astype(BF16)
        y_ref[...] = jnp.dot(mid, wd_ref[0, 0], preferred_element_type=F32)

    @pl.when(i >= nu_ref[0])
    def _():
        y_ref[...] = jnp.zeros_like(y_ref)


def _moe_experts(xs, blk_e, n_used, layer, wg, wu, wd, blk, nblocks):
    d = xs.shape[1]
    de = wg.shape[-1]
    grid_spec = pltpu.PrefetchScalarGridSpec(
        num_scalar_prefetch=3,
        grid=(nblocks,),
        in_specs=[pl.BlockSpec((blk, d), lambda i, be, nu, ly: (i, 0)),
                  pl.BlockSpec((1, 1, d, de), lambda i, be, nu, ly: (ly[0], be[i], 0, 0)),
                  pl.BlockSpec((1, 1, d, de), lambda i, be, nu, ly: (ly[0], be[i], 0, 0)),
                  pl.BlockSpec((1, 1, de, d), lambda i, be, nu, ly: (ly[0], be[i], 0, 0))],
        out_specs=pl.BlockSpec((blk, d), lambda i, be, nu, ly: (i, 0)),
    )
    return pl.pallas_call(
        _expert_kernel,
        grid_spec=grid_spec,
        out_shape=jax.ShapeDtypeStruct((nblocks * blk, d), F32),
        compiler_params=_params(("arbitrary",)),
        name="moe_experts",
    )(blk_e, n_used, layer, xs, wg, wu, wd)


def _combine_kernel(dest_ref, h_ref, route_ref, g_ref, b_ref, ys_hbm, o_ref, ob_ref, gbuf, sem, *, tb, alpha):
    def copy(r, s):
        d = dest_ref[0, 0, 2 * r + s]
        return pltpu.make_async_copy(ys_hbm.at[pl.ds(d, 1)], gbuf.at[s, pl.ds(r, 1)], sem)

    def start(r):
        copy(r, 0).start()
        copy(r, 1).start()

    def wait(r):
        copy(r, 0).wait()
        copy(r, 1).wait()

    _for_row_groups(tb, start)
    _for_row_groups(tb, wait)
    route = route_ref[...]
    moe = route[:, 2:3] * gbuf[0] + route[:, 3:4] * gbuf[1]
    y = _layer_norm_val(alpha * h_ref[...] + moe, g_ref[...], b_ref[...])
    o_ref[...] = y
    ob_ref[...] = y.astype(BF16)


def _moe_combine(h1, route, dest_idx, ys, g, b, alpha, tb):
    T, d = h1.shape
    nblk = dest_idx.shape[0]
    row = lambda w: pl.BlockSpec((tb, w), lambda i: (i, 0))
    return pl.pallas_call(
        functools.partial(_combine_kernel, tb=tb, alpha=alpha),
        grid=(nblk,),
        in_specs=[pl.BlockSpec((1, 1, 2 * tb), lambda i: (i, 0, 0), memory_space=pltpu.SMEM),
                  row(d), row(LANES), _const_spec((1, d)), _const_spec((1, d)),
                  pl.BlockSpec(memory_space=pl.ANY)],
        out_specs=[row(d), row(d)],
        out_shape=[jax.ShapeDtypeStruct((T, d), F32), jax.ShapeDtypeStruct((T, d), BF16)],
        scratch_shapes=[pltpu.VMEM((2, tb, d), F32), pltpu.SemaphoreType.DMA(())],
        compiler_params=_params(("arbitrary",)),
        name="moe_combine_ln",
    )(dest_idx, h1, route, g.reshape(1, d), b.reshape(1, d), ys)


def _moe(h1, route, layer, wg, wu, wd, g, b, alpha):
    T, d = h1.shape
    blk = MOE_BLOCK
    tb = MOE_TOKEN_BLOCK
    dest, blkinfo, nblocks = _dispatch(route, blk)
    nblk_tok = pl.cdiv(T, tb)
    di = dest[:, 0:2].astype(I32)
    dest_idx = jnp.pad(di, ((0, nblk_tok * tb - T), (0, 0))).reshape(nblk_tok, 1, 2 * tb)
    blk_e = blkinfo[:nblocks, 0].astype(I32)
    n_used = blkinfo[0:1, 1].astype(I32)
    xs = _moe_scatter(h1, dest_idx, nblocks * blk, tb)
    ys = _moe_experts(xs, blk_e, n_used, layer, wg, wu, wd, blk, nblocks)
    return _moe_combine(h1, route, dest_idx, ys, g, b, alpha, tb)


def _prep_weights(W):
    depth, d, _ = W["w_in"].shape
    dr = W["conv_a_w"].shape[-1]
    dh = W["conv_b_w"].shape[-1] // 3
    ql, kvl = W["q_norm_g"].shape[-1], W["kv_norm_g"].shape[-1]
    ca, cbw = 2 * dr, 3 * dh
    cc = ql + kvl + QK_ROPE
    w_in = W["w_in"]
    half = QK_ROPE // 2

    def rot(w):
        return jnp.concatenate([-w[..., half:], w[..., :half]], axis=-1)

    w_a = w_in[..., :ca]
    w_b = w_in[..., ca:ca + cbw]
    w_c = w_in[..., ca + cbw:ca + cbw + cc]
    w_g = w_in[..., ca + cbw + cc:]
    w_kr = w_c[..., ql + kvl:]
    zpad = jnp.zeros(w_kr.shape[:-1] + (LANES - QK_ROPE,), F32)
    w_c_ext = jnp.concatenate([w_c[..., :ql + kvl], w_kr, zpad, rot(w_kr), zpad], axis=-1)

    wq = W["w_uq"].reshape(depth, ql, N_HEADS, QK_NOPE + QK_ROPE)
    z_n = jnp.zeros((depth, ql, N_HEADS, QK_NOPE), F32)
    z_p = jnp.zeros((depth, ql, N_HEADS, HEAD_PAD - QK_NOPE - QK_ROPE), F32)
    wa = jnp.concatenate([wq, z_p], axis=-1).reshape(depth, ql, N_HEADS * HEAD_PAD)
    wb = jnp.concatenate([z_n, rot(wq[..., QK_NOPE:]), z_p], axis=-1).reshape(depth, ql, N_HEADS * HEAD_PAD)
    wkv = W["w_ukv"].reshape(depth, kvl, N_HEADS, QK_NOPE + V_DIM)
    wkn = wkv[..., :QK_NOPE].reshape(depth, kvl, N_HEADS * QK_NOPE)
    wv = wkv[..., QK_NOPE:].reshape(depth, kvl, N_HEADS * V_DIM)

    ng, ne = N_GROUPS, N_GROUPS * EXPERTS_PER_GROUP
    wr = jnp.concatenate([W["w_rg"], W["w_re"], jnp.zeros((depth, d, LANES - ng - ne), F32)], axis=-1)
    br = jnp.concatenate([W["b_rg"], W["b_re"], jnp.zeros((depth, LANES - ng - ne), F32)], axis=-1)
    wr_hi = wr.astype(BF16)
    wr_lo = (wr - wr_hi.astype(F32)).astype(BF16)
    bf = lambda x: x.astype(BF16)
    return dict(
        w_a=bf(w_a), w_b=bf(w_b), w_c=bf(w_c_ext), w_g=bf(w_g),
        conv_a_w=W["conv_a_w"], conv_a_b=W["conv_a_b"], rg_wa=bf(W["rg_wa"]), rg_ba=W["rg_ba"],
        rg_wx=bf(W["rg_wx"]), rg_bx=W["rg_bx"], rg_lambda=W["rg_lambda"],
        conv_b_w=W["conv_b_w"], conv_b_b=W["conv_b_b"], hy_w1=W["hy_w1"], hy_b1=W["hy_b1"],
        hy_w2=W["hy_w2"], hy_b2=W["hy_b2"], hy_w3=W["hy_w3"], hy_freq=W["hy_freq"], hy_skip=W["hy_skip"],
        q_norm_g=W["q_norm_g"], kv_norm_g=W["kv_norm_g"], wa=bf(wa), wb=bf(wb), wkn=bf(wkn), wv=bf(wv),
        w_pa=bf(W["w_pa"]), w_pb=bf(W["w_pb"]), w_pc=bf(W["w_pc"]), w_out=bf(W["w_out"]),
        ln1_g=W["ln1_g"], ln1_b=W["ln1_b"], wr_hi=wr_hi, wr_lo=wr_lo, br=br.reshape(depth, 1, LANES),
        w_e_gate=bf(W["w_e_gate"]), w_e_up=bf(W["w_e_up"]), w_e_down=bf(W["w_e_down"]),
        ln2_g=W["ln2_g"], ln2_b=W["ln2_b"],
    )


def _encoder_layer(h, hb, P, experts, B, L, plan, tabs, rope_tab, alpha):
    T = B * L
    pa = _matmul(hb, P["w_a"], BF16, "in_proj_a")
    pb = _matmul(hb, P["w_b"], BF16, "in_proj_b")
    pc = _matmul(hb, P["w_c"], BF16, "in_proj_c")
    pg = _matmul(hb, P["w_g"], BF16, "in_proj_g")

    a, u = _rglru_gates(pa.reshape(B, L, -1), P["conv_a_w"], P["conv_a_b"], P["rg_wa"], P["rg_ba"],
                        P["rg_wx"], P["rg_bx"], P["rg_lambda"])
    h2 = _rglru_scan(a, u)

    s, x0 = _hy_conv3(pb.reshape(B, L, -1), P["conv_b_w"], P["conv_b_b"], plan)
    dh = s.shape[-1]
    hl, norm = _hy_taps(L, plan, P["hy_w1"], P["hy_b1"], P["hy_w2"], P["hy_b2"], P["hy_w3"], P["hy_freq"])
    f_r, f_i = _hy_filter_fft(hl, norm, plan, tabs)
    x_r, x_i = _hy_fwd_fft(s.reshape(B * plan.nb, plan.ns, dh), plan, tabs)
    y_r, y_i = _hy_mac(x_r, x_i, f_r, f_i, B, plan)
    y = _hy_inv_fft(y_r, y_i, plan, tabs).reshape(B, plan.nb * plan.ns, dh)
    if plan.nb * plan.ns != L and B > 1:
        y, s, x0 = y[:, :L], s[:, :L], x0[:, :L]
    flat = lambda t: t.reshape(-1, dh)

    q, k, v, kmax = _mla_prep(pc.reshape(B, L, -1), rope_tab, P["q_norm_g"], P["kv_norm_g"],
                        P["wa"], P["wb"], P["wkn"], P["wv"])
    o = _attention(q, k, v, kmax).reshape(T, -1)

    m = _merge(h2, pa, flat(y), flat(s), flat(x0), P["hy_skip"], o, pg, P["w_pa"], P["w_pb"], P["w_pc"])
    h1, route = _outproj(m, h, P["w_out"], P["ln1_g"], P["ln1_b"], P["wr_hi"], P["wr_lo"], P["br"], alpha)
    return _moe(h1, route, P["layer"], experts["w_e_gate"], experts["w_e_up"], experts["w_e_down"],
                P["ln2_g"], P["ln2_b"], alpha)


def _encoder_trunk(x, meta, ln_g, ln_b, PW, alpha):
    B, S, D = x.shape
    L = S + N_META
    xc = jnp.concatenate([jnp.broadcast_to(meta[None], (B, N_META, D)), x], axis=1).reshape(B * L, D)
    h, hb = _layer_norm_rows(xc, ln_g, ln_b)
    plan = _FftPlan(L)
    tabs = plan.tables()
    rope_tab = _rope_tables(L)
    expert_names = ("w_e_gate", "w_e_up", "w_e_down")
    experts = {k: PW[k] for k in expert_names}
    per_layer = {k: v for k, v in PW.items() if k not in expert_names}
    per_layer["layer"] = jnp.arange(PW["w_a"].shape[0], dtype=I32).reshape(-1, 1)

    def body(carry, P):
        return _encoder_layer(carry[0], carry[1], P, experts, B, L, plan, tabs, rope_tab, alpha), None

    (h, hb), _ = lax.scan(body, (h, hb), per_layer)
    return h.reshape(B, L, D)[:, N_META:]


def kernel(x_prompt, x_sample, meta, ln_emb_g, ln_emb_b, w_in, conv_a_w, conv_a_b, rg_wa, rg_ba, rg_wx, rg_bx, rg_lambda, conv_b_w, conv_b_b, hy_w1, hy_b1, hy_w2, hy_b2, hy_w3, hy_freq, hy_skip, q_norm_g, w_uq, kv_norm_g, w_ukv, w_pa, w_pb, w_pc, w_out, ln1_g, ln1_b, w_rg, b_rg, w_re, b_re, w_e_gate, w_e_up, w_e_down, ln2_g, ln2_b):
    W = dict(w_in=w_in, conv_a_w=conv_a_w, conv_a_b=conv_a_b, rg_wa=rg_wa, rg_ba=rg_ba, rg_wx=rg_wx,
             rg_bx=rg_bx, rg_lambda=rg_lambda, conv_b_w=conv_b_w, conv_b_b=conv_b_b, hy_w1=hy_w1,
             hy_b1=hy_b1, hy_w2=hy_w2, hy_b2=hy_b2, hy_w3=hy_w3, hy_freq=hy_freq, hy_skip=hy_skip,
             q_norm_g=q_norm_g, w_uq=w_uq, kv_norm_g=kv_norm_g, w_ukv=w_ukv, w_pa=w_pa, w_pb=w_pb,
             w_pc=w_pc, w_out=w_out, ln1_g=ln1_g, ln1_b=ln1_b, w_rg=w_rg, b_rg=b_rg, w_re=w_re, b_re=b_re,
             w_e_gate=w_e_gate, w_e_up=w_e_up, w_e_down=w_e_down, ln2_g=ln2_g, ln2_b=ln2_b)
    depth = w_in.shape[0]
    alpha = (2 * depth) ** 0.25
    PW = _prep_weights(W)
    y_prompt = _encoder_trunk(x_prompt, meta, ln_emb_g, ln_emb_b, PW, alpha)
    y_sample = _encoder_trunk(x_sample, meta, ln_emb_g, ln_emb_b, PW, alpha)
    return (y_prompt, y_sample)
```

```python
import functools
import math

import numpy as np
import jax
import jax.numpy as jnp
from jax import lax
from jax.experimental import pallas as pl
from jax.experimental.pallas import tpu as pltpu

F32 = jnp.float32
BF16 = jnp.bfloat16
I32 = jnp.int32

N_META = 16
RG_C = 8.0
HY_BANDS = 16
HY_FAST_DECAY = 0.3
HY_SLOW_DECAY = 1.5
HY_TARGET = 1e-2
N_HEADS = 16
QK_NOPE = 128
QK_ROPE = 64
V_DIM = 128
ROPE_THETA = 10000.0
N_GROUPS = 8
EXPERTS_PER_GROUP = 8
LN_EPS = 1e-5
RMS_EPS = 1e-6

V7X_VMEM_BYTES = 64 * 1024 * 1024
VMEM_LIMIT = 56 * 1024 * 1024
LANES = 128
HEAD_PAD = 2 * LANES

ROW_BLOCK = 512
MM_ROW_BLOCK = 1024
HALO = 16
HY_MAX_BLOCK = 4608
MOE_BLOCK = 256
MOE_TOKEN_BLOCK = 256
MERGE_ROW_BLOCK = 256
DMA_GROUP = 8
ATT_TQ = 384
ATT_CK = 512
ATT_UNROLL = 16
ATT_EXACT_UNROLL = 2
ATT_MIN_DENOM = 2.0 ** -60
DFT_UNROLL = 8
SCAN_BATCH_GROUP = 4
SCAN_BLOCK_BYTES = 9 * 512 * 1024


def _params(sem, vmem=VMEM_LIMIT):
    return pltpu.CompilerParams(dimension_semantics=sem, vmem_limit_bytes=vmem)


def _round_up(x, m):
    return (x + m - 1) // m * m


def _pick_tile(n, cands):
    for c in cands:
        if n % c == 0:
            return c
    return n


def _const_spec(shape):
    nd = len(shape)
    return pl.BlockSpec(shape, lambda *a: (0,) * nd, pipeline_mode=pl.Buffered(1))


def _layer_norm_val(x, g, b):
    mu = jnp.mean(x, axis=-1, keepdims=True)
    xc = x - mu
    var = jnp.mean(xc * xc, axis=-1, keepdims=True)
    return xc * lax.rsqrt(var + LN_EPS) * g + b


def _ln_kernel(x_ref, g_ref, b_ref, o_ref, ob_ref):
    y = _layer_norm_val(x_ref[...], g_ref[...], b_ref[...])
    o_ref[...] = y
    ob_ref[...] = y.astype(BF16)


def _layer_norm_rows(x, g, b):
    T, D = x.shape
    rb = ROW_BLOCK
    row = pl.BlockSpec((rb, D), lambda i: (i, 0))
    vec = pl.BlockSpec((1, D), lambda i: (0, 0))
    return pl.pallas_call(
        _ln_kernel,
        grid=(pl.cdiv(T, rb),),
        in_specs=[row, vec, vec],
        out_specs=[row, row],
        out_shape=[jax.ShapeDtypeStruct((T, D), F32), jax.ShapeDtypeStruct((T, D), BF16)],
        compiler_params=_params(("parallel",)),
        name="embed_ln",
    )(x, g.reshape(1, D), b.reshape(1, D))


def _mm_kernel(x_ref, w_ref, o_ref):
    o_ref[...] = jnp.dot(x_ref[...], w_ref[...], preferred_element_type=F32).astype(o_ref.dtype)


def _matmul(x, w, out_dtype, name):
    T, K = x.shape
    N = w.shape[1]
    rb = MM_ROW_BLOCK
    tn = _pick_tile(N, (1024, 768, 512, 256, 128))
    return pl.pallas_call(
        _mm_kernel,
        grid=(pl.cdiv(T, rb), N // tn),
        in_specs=[pl.BlockSpec((rb, K), lambda i, j: (i, 0)),
                  pl.BlockSpec((K, tn), lambda i, j: (0, j))],
        out_specs=pl.BlockSpec((rb, tn), lambda i, j: (i, j)),
        out_shape=jax.ShapeDtypeStruct((T, N), out_dtype),
        compiler_params=_params(("parallel", "arbitrary")),
        name=name,
    )(x, w)


def _halo_specs(tb, cb, n_in_blocks, n16, col_of):
    r = tb // HALO
    main = pl.BlockSpec((1, tb, cb), lambda b, i, c: (b, jnp.minimum(i, n_in_blocks - 1), col_of(c)))
    prev = pl.BlockSpec((1, HALO, cb), lambda b, i, c: (b, jnp.clip(i * r - 1, 0, n16 - 1), col_of(c)))
    nxt = pl.BlockSpec((1, HALO, cb), lambda b, i, c: (b, jnp.clip((i + 1) * r, 0, n16 - 1), col_of(c)))
    return [main, prev, nxt]


def _halo_window(xm, xp, xn, i, tb, seq_len):
    w = jnp.concatenate([xp[0], xm[0], xn[0]], axis=0).astype(F32)
    t = i * tb - HALO + lax.broadcasted_iota(I32, (tb + 2 * HALO, 1), 0)
    return jnp.where((t >= 0) & (t < seq_len), w, 0.0)


def _rglru_gate_kernel(xm, xp, xn, cw, cb, wa, ba, wx, bx, lam, a_ref, u_ref, *, tb, seq_len):
    i = pl.program_id(1)
    w = _halo_window(xm, xp, xn, i, tb, seq_len)
    o = HALO
    xb = (cw[0:1] * w[o - 1:o - 1 + tb] + cw[1:2] * w[o:o + tb]
          + cw[2:3] * w[o + 1:o + 1 + tb] + cw[3:4] * w[o + 2:o + 2 + tb] + cb[...])
    for k in range(xb.shape[1] // LANES):
        cols = slice(k * LANES, (k + 1) * LANES)
        xk = xb[:, cols]
        xkb = xk.astype(BF16)
        for d in range(2):
            r = jax.nn.sigmoid(jnp.dot(xkb, wa[d, k], preferred_element_type=F32) + ba[d:d + 1, cols])
            ig = jax.nn.sigmoid(jnp.dot(xkb, wx[d, k], preferred_element_type=F32) + bx[d:d + 1, cols])
            z = -lam[d:d + 1, cols]
            softplus = jnp.maximum(z, 0.0) + jnp.log(1.0 + jnp.exp(-jnp.abs(z)))
            log_a = -RG_C * r * softplus
            a_ref[d, 0, :, cols] = jnp.exp(log_a)
            u_ref[d, 0, :, cols] = jnp.sqrt(1.0 - jnp.exp(2.0 * log_a)) * (ig * xk)


def _rglru_gates(pa3, conv_w, conv_b, wa, ba, wx, bx, lam):
    B, L, _ = pa3.shape
    DR = conv_w.shape[1]
    tb = ROW_BLOCK
    nt = pl.cdiv(L, tb)
    n16 = L // HALO
    vec = lambda rows: pl.BlockSpec((rows, DR), lambda b, i, k: (0, 0))
    wspec = pl.BlockSpec(wa.shape, lambda b, i, k: (0, 0, 0, 0))
    out = pl.BlockSpec((2, 1, tb, DR), lambda b, i, k: (0, b, i, 0))
    return pl.pallas_call(
        functools.partial(_rglru_gate_kernel, tb=tb, seq_len=L),
        grid=(B, nt, 1),
        in_specs=_halo_specs(tb, DR, nt, n16, lambda k: 0)
        + [vec(4), vec(1), wspec, vec(2), wspec, vec(2), vec(2)],
        out_specs=[out, out],
        out_shape=[jax.ShapeDtypeStruct((2, B, L, DR), F32)] * 2,
        compiler_params=_params(("parallel", "parallel", "arbitrary")),
        name="rglru_gates",
    )(pa3, pa3, pa3, conv_w, conv_b.reshape(1, DR), wa, ba, wx, bx, lam)


def _scan_kernel(a_ref, u_ref, o_ref, h_ref, *, steps):
    d = pl.program_id(0)
    c = pl.program_id(2)

    @pl.when(c == 0)
    def _():
        h_ref[...] = jnp.zeros_like(h_ref)

    def body(i, h):
        t = i + d * (steps - 1 - 2 * i)
        h = a_ref[0, :, t] * h + u_ref[0, :, t]
        o_ref[0, :, t] = h
        return h

    h_ref[...] = lax.fori_loop(0, steps, body, h_ref[...], unroll=4)


def _rglru_scan(a, u):
    _, B, L, DR = a.shape
    r = DR // LANES
    a5 = a.reshape(2, B, L, r, LANES)
    u5 = u.reshape(2, B, L, r, LANES)
    bg = SCAN_BATCH_GROUP if B % SCAN_BATCH_GROUP == 0 else B
    budget = SCAN_BLOCK_BYTES // (bg * r * LANES * 4)
    steps = max(s for s in range(1, L + 1) if L % s == 0 and s <= max(budget, 1))
    nch = L // steps
    spec = pl.BlockSpec((1, bg, steps, r, LANES), lambda d, g, c: (d, g, c + d * (nch - 1 - 2 * c), 0, 0))
    h = pl.pallas_call(
        functools.partial(_scan_kernel, steps=steps),
        grid=(2, B // bg, nch),
        in_specs=[spec, spec],
        out_specs=spec,
        out_shape=jax.ShapeDtypeStruct((2, B, L, r, LANES), F32),
        scratch_shapes=[pltpu.VMEM((bg, r, LANES), F32)],
        compiler_params=_params(("arbitrary", "arbitrary", "arbitrary")),
        name="rglru_scan",
    )(a5, u5)
    return h.reshape(2, B * L, DR)


class _FftPlan:
    def __init__(self, seq_len):
        self.nb = -(-seq_len // HY_MAX_BLOCK)
        self.ns = _round_up(-(-seq_len // self.nb), 16)
        best = None
        for n1 in (30, 62, 94, 126):
            n2 = _round_up(-(-(2 * self.ns - 1) // n1), 16)
            cost = n1 * n2 * (n1 + n2)
            if best is None or cost < best[0]:
                best = (cost, n1, n2)
        _, self.n1, self.n2 = best
        self.n = self.n1 * self.n2
        self.k1 = self.n1 // 2 + 1
        self.j1 = _round_up(-(-self.ns // self.n2), 16)
        self.j1f = _round_up(self.n1, 16)
        self.rows = self.k1 * self.n2

    def tables(self):
        n1, n2, n, k1n = self.n1, self.n2, self.n, self.k1
        k1 = np.arange(k1n, dtype=np.float64)[None, :, None]
        j2 = np.arange(n2, dtype=np.float64)[:, None, None]

        def fwd(jcols):
            j1 = np.arange(jcols, dtype=np.float64)[None, None, :]
            ang = 2.0 * np.pi * k1 * (j1 * n2 + j2) / n
            live = (j1 < n1)
            return np.concatenate([np.cos(ang) * live, -np.sin(ang) * live], axis=1)

        g_sig = fwd(self.j1)
        g_filt = fwd(self.j1f)
        a = np.arange(n2, dtype=np.float64)
        phi = 2.0 * np.pi * np.outer(a, a) / n2
        c, s = np.cos(phi), np.sin(phi)
        m2f_r = np.concatenate([c, -s], axis=0)
        m2f_i = np.concatenate([s, c], axis=0)
        m2i_r = np.concatenate([c, s], axis=0)
        m2i_i = np.concatenate([-s, c], axis=0)
        j1 = np.arange(self.j1, dtype=np.float64)[None, :, None]
        kk = np.arange(k1n, dtype=np.float64)[None, None, :]
        ang = 2.0 * np.pi * kk * (j1 * n2 + j2) / n
        coef = np.where((kk == 0) | (kk == n1 // 2), 1.0, 2.0) / n
        gi_r = coef * np.cos(ang)
        gi_i = -coef * np.sin(ang)
        bf = lambda x: jnp.asarray(x, dtype=F32).astype(BF16)
        return dict(g_sig=bf(g_sig), g_filt=bf(g_filt), m2f_r=bf(m2f_r), m2f_i=bf(m2f_i),
                    m2i_r=bf(m2i_r), m2i_i=bf(m2i_i), gi_r=bf(gi_r), gi_i=bf(gi_i))


def _hy_conv3_kernel(*refs, tb, seq_len):
    x_refs = refs[0:9]
    cw = refs[9:12]
    cb = refs[12:15]
    s_ref, x0_ref = refs[15], refs[16]
    i = pl.program_id(1)
    o = HALO
    outs = []
    for p in range(3):
        w = _halo_window(x_refs[3 * p], x_refs[3 * p + 1], x_refs[3 * p + 2], i, tb, seq_len)
        outs.append(cw[p][0:1] * w[o - 1:o - 1 + tb] + cw[p][1:2] * w[o:o + tb]
                    + cw[p][2:3] * w[o + 1:o + 1 + tb] + cb[p][...])
    t = i * tb + lax.broadcasted_iota(I32, (tb, 1), 0)
    live = t < seq_len
    s_ref[0] = jnp.where(live, outs[1] * outs[2], 0.0)
    x0_ref[0] = jnp.where(live, outs[0], 0.0).astype(BF16)


def _hy_conv3(pb3, conv_w, conv_b, plan):
    B, L, _ = pb3.shape
    DH = conv_w.shape[1] // 3
    cb_ = _pick_tile(DH, (1024, 512, 256, 128))
    nc = DH // cb_
    tb = ROW_BLOCK
    lp = plan.nb * plan.ns
    nt_in = pl.cdiv(L, tb)
    n16 = L // HALO
    in_specs = []
    for p in range(3):
        in_specs += _halo_specs(tb, cb_, nt_in, n16, lambda c, p=p: p * nc + c)
    in_specs += [pl.BlockSpec((3, cb_), lambda b, i, c, p=p: (0, p * nc + c)) for p in range(3)]
    in_specs += [pl.BlockSpec((1, cb_), lambda b, i, c, p=p: (0, p * nc + c)) for p in range(3)]
    out = pl.BlockSpec((1, tb, cb_), lambda b, i, c: (b, i, c))
    return pl.pallas_call(
        functools.partial(_hy_conv3_kernel, tb=tb, seq_len=L),
        grid=(B, pl.cdiv(lp, tb), nc),
        in_specs=in_specs,
        out_specs=[out, out],
        out_shape=[jax.ShapeDtypeStruct((B, lp, DH), F32), jax.ShapeDtypeStruct((B, lp, DH), BF16)],
        compiler_params=_params(("parallel", "parallel", "parallel")),
        name="hyena_conv3",
    )(*([pb3] * 9), *([conv_w] * 3), *([conv_b.reshape(1, 3 * DH)] * 3))


def _hy_lag_table(seq_len, plan):
    half = plan.nb * plan.ns
    rows = _round_up(2 * half, ROW_BLOCK)
    lag = np.arange(rows) - half
    pos = np.abs(lag)
    valid = (np.arange(rows) < 2 * half) & (pos <= seq_len - 1)
    posc = np.minimum(pos, seq_len - 1)
    t01 = np.linspace(0.0, 1.0, seq_len, dtype=np.float32)
    ang = ((2.0 * math.pi / seq_len) * np.arange(seq_len, dtype=np.float32)).astype(np.float32)
    bands = np.linspace(1e-4, HY_BANDS - 1, HY_BANDS, dtype=np.float32)
    arg = (bands[None, :] * ang[:, None]).astype(np.float64)
    emb = np.concatenate([t01[:, None].astype(np.float64), np.cos(arg), -np.sin(arg)], axis=-1)
    table = np.zeros((rows, LANES), np.float32)
    ne = emb.shape[1]
    table[:, :ne] = emb[posc] * valid[:, None]
    table[:, ne] = valid
    table[:, ne + 1] = lag >= 0
    return jnp.asarray(table), ne


def _hy_taps_kernel(e_ref, w1, b1, w2, b2, w3, fr, rates, hl_ref, norm_ref, *, ne, dh):
    hi = lax.Precision.HIGHEST
    e = e_ref[...]
    f = fr[...]
    z = jnp.sin(f * (jnp.dot(e, w1[...], preferred_element_type=F32, precision=hi) + b1[...]))
    z = jnp.sin(f * (jnp.dot(z, w2[...], preferred_element_type=F32, precision=hi) + b2[...]))
    z3 = jnp.dot(z, w3[...], preferred_element_type=F32, precision=hi)
    t01 = e[:, 0:1]
    valid = e[:, ne:ne + 1]
    is_fwd = e[:, ne + 1:ne + 2]
    decay = jnp.exp(-t01 * rates[...])
    taps = jnp.where(is_fwd > 0.5, z3[:, :dh], z3[:, dh:]) * decay * valid
    hl_ref[...] = taps

    @pl.when(pl.program_id(0) == 0)
    def _():
        norm_ref[...] = jnp.zeros_like(norm_ref)

    norm_ref[0:1, :] += jnp.sum(jnp.abs(taps), axis=0, keepdims=True)


def _hy_taps(seq_len, plan, w1, b1, w2, b2, w3, freq):
    table, ne = _hy_lag_table(seq_len, plan)
    rows = table.shape[0]
    nf = w1.shape[1]
    dh = w3.shape[1] // 2
    w1p = jnp.zeros((LANES, nf), F32).at[:ne].set(w1)
    rates = np.abs(np.linspace(math.log(HY_TARGET) / HY_SLOW_DECAY, math.log(HY_TARGET) / HY_FAST_DECAY,
                               dh, dtype=np.float32)).reshape(1, dh)
    rb = ROW_BLOCK
    full = lambda a: pl.BlockSpec(a.shape, lambda i: (0,) * a.ndim)
    args = (table, w1p, b1.reshape(1, nf), w2, b2.reshape(1, nf), w3, freq.reshape(1, nf), jnp.asarray(rates))
    return pl.pallas_call(
        functools.partial(_hy_taps_kernel, ne=ne, dh=dh),
        grid=(rows // rb,),
        in_specs=[pl.BlockSpec((rb, LANES), lambda i: (i, 0))] + [full(a) for a in args[1:]],
        out_specs=[pl.BlockSpec((rb, dh), lambda i: (i, 0)), pl.BlockSpec((8, dh), lambda i: (0, 0))],
        out_shape=[jax.ShapeDtypeStruct((rows, dh), F32), jax.ShapeDtypeStruct((8, dh), F32)],
        compiler_params=_params(("arbitrary",)),
        name="hyena_taps",
    )(*args)


def _dft_forward(src_ref, g_ref, m2r_ref, m2i_ref, xr, xi, out_r, out_i, scale, plan, jcols):
    n2, k1n = plan.n2, plan.k1

    def stage1(j2, c):
        v = src_ref[pl.ds(j2, jcols, stride=n2), :].astype(BF16)
        p = jnp.dot(g_ref[j2], v, preferred_element_type=F32)
        st = pl.multiple_of(j2 * k1n, 16)
        xr[pl.ds(st, k1n), :] = p[:k1n]
        xi[pl.ds(st, k1n), :] = p[k1n:]
        return c

    lax.fori_loop(0, n2, stage1, 0, unroll=DFT_UNROLL)

    def stage2(k, c):
        st = pl.multiple_of(k * n2, 16)
        yr = xr[pl.ds(k, n2, stride=k1n), :].astype(BF16)
        yi = xi[pl.ds(k, n2, stride=k1n), :].astype(BF16)
        z = (jnp.dot(m2r_ref[...], yr, preferred_element_type=F32)
             + jnp.dot(m2i_ref[...], yi, preferred_element_type=F32))
        if scale is not None:
            z = z * scale
        out_r[0, pl.ds(st, n2), :] = z[:n2].astype(BF16)
        out_i[0, pl.ds(st, n2), :] = z[n2:].astype(BF16)
        return c

    lax.fori_loop(0, k1n, stage2, 0, unroll=DFT_UNROLL)


def _hy_filter_fft_kernel(hi_ref, lo_ref, norm_ref, g_ref, m2r_ref, m2i_ref, hr_ref, hi_out_ref,
                          fbuf, xr, xi, *, plan):
    ns, n = plan.ns, plan.n
    rows = fbuf.shape[0]
    fbuf[pl.ds(0, ns), :] = hi_ref[...]
    fbuf[pl.ds(ns, rows - ns), :] = jnp.zeros((rows - ns, fbuf.shape[1]), F32)
    fbuf[pl.ds(n - ns, ns), :] = lo_ref[...]
    fbuf[pl.ds(n - ns, 1), :] = jnp.zeros((1, fbuf.shape[1]), F32)
    inv_norm = 1.0 / norm_ref[0:1, :]
    _dft_forward(fbuf, g_ref, m2r_ref, m2i_ref, xr, xi, hr_ref, hi_out_ref, inv_norm, plan, plan.j1f)


def _hy_filter_fft(hl, norm, plan, tabs):
    dh = hl.shape[1]
    cb_ = LANES
    nbd = 2 * plan.nb - 1
    ns = plan.ns
    spec_out = pl.BlockSpec((1, plan.rows, cb_), lambda p, c: (p, 0, c))
    return pl.pallas_call(
        functools.partial(_hy_filter_fft_kernel, plan=plan),
        grid=(nbd, dh // cb_),
        in_specs=[pl.BlockSpec((ns, cb_), lambda p, c: (p + 1, c)),
                  pl.BlockSpec((ns, cb_), lambda p, c: (p, c)),
                  pl.BlockSpec((8, cb_), lambda p, c: (0, c)),
                  _const_spec(tabs["g_filt"].shape), _const_spec(tabs["m2f_r"].shape),
                  _const_spec(tabs["m2f_i"].shape)],
        out_specs=[spec_out, spec_out],
        out_shape=[jax.ShapeDtypeStruct((nbd, plan.rows, dh), BF16)] * 2,
        scratch_shapes=[pltpu.VMEM((plan.j1f * plan.n2, cb_), F32),
                        pltpu.VMEM((plan.rows, cb_), F32), pltpu.VMEM((plan.rows, cb_), F32)],
        compiler_params=_params(("parallel", "parallel")),
        name="hyena_filter_fft",
    )(hl, hl, norm, tabs["g_filt"], tabs["m2f_r"], tabs["m2f_i"])


def _hy_fwd_fft_kernel(s_ref, g_ref, m2r_ref, m2i_ref, xr_out, xi_out, sbuf, xr, xi, *, plan):
    ns = plan.ns
    rows = sbuf.shape[0]
    sbuf[pl.ds(0, ns), :] = s_ref[0]
    if rows > ns:
        sbuf[pl.ds(ns, rows - ns), :] = jnp.zeros((rows - ns, sbuf.shape[1]), F32)
    _dft_forward(sbuf, g_ref, m2r_ref, m2i_ref, xr, xi, xr_out, xi_out, None, plan, plan.j1)


def _hy_fwd_fft(s_blocks, plan, tabs):
    q, ns, dh = s_blocks.shape
    cb_ = LANES
    spec_out = pl.BlockSpec((1, plan.rows, cb_), lambda b, c: (b, 0, c))
    return pl.pallas_call(
        functools.partial(_hy_fwd_fft_kernel, plan=plan),
        grid=(q, dh // cb_),
        in_specs=[pl.BlockSpec((1, ns, cb_), lambda b, c: (b, 0, c)),
                  _const_spec(tabs["g_sig"].shape), _const_spec(tabs["m2f_r"].shape),
                  _const_spec(tabs["m2f_i"].shape)],
        out_specs=[spec_out, spec_out],
        out_shape=[jax.ShapeDtypeStruct((q, plan.rows, dh), BF16)] * 2,
        scratch_shapes=[pltpu.VMEM((plan.j1 * plan.n2, cb_), F32),
                        pltpu.VMEM((plan.rows, cb_), F32), pltpu.VMEM((plan.rows, cb_), F32)],
        compiler_params=_params(("parallel", "parallel")),
        name="hyena_fwd_fft",
    )(s_blocks, tabs["g_sig"], tabs["m2f_r"], tabs["m2f_i"])


def _hy_mac_kernel(xr_ref, xi_ref, hr_ref, hi_ref, yr_ref, yi_ref, *, nb):
    for a in range(nb):
        yr = None
        for b in range(nb):
            d = a - b + nb - 1
            xr, xi = xr_ref[0, b].astype(F32), xi_ref[0, b].astype(F32)
            hr, hi = hr_ref[d].astype(F32), hi_ref[d].astype(F32)
            pr = xr * hr - xi * hi
            pi = xr * hi + xi * hr
            yr, yi = (pr, pi) if yr is None else (yr + pr, yi + pi)
        yr_ref[0, a] = yr.astype(BF16)
        yi_ref[0, a] = yi.astype(BF16)


def _hy_mac(xr, xi, hr, hi, batch, plan):
    nb = plan.nb
    dh = xr.shape[-1]
    rows = plan.rows
    rc = _pick_tile(rows, (256, 128, 64, 32, 16))
    cc = _pick_tile(dh, (1024, 512, 256, 128))
    x4 = lambda a: a.reshape(batch, nb, rows, dh)
    xs = pl.BlockSpec((1, nb, rc, cc), lambda b, r, c: (b, 0, r, c))
    hs = pl.BlockSpec((2 * nb - 1, rc, cc), lambda b, r, c: (0, r, c))
    yr, yi = pl.pallas_call(
        functools.partial(_hy_mac_kernel, nb=nb),
        grid=(batch, rows // rc, dh // cc),
        in_specs=[xs, xs, hs, hs],
        out_specs=[xs, xs],
        out_shape=[jax.ShapeDtypeStruct((batch, nb, rows, dh), BF16)] * 2,
        compiler_params=_params(("parallel", "parallel", "parallel")),
        name="hyena_freq_mac",
    )(x4(xr), x4(xi), hr, hi)
    return yr.reshape(batch * nb, rows, dh), yi.reshape(batch * nb, rows, dh)


def _hy_inv_fft_kernel(*refs, plan, with_filter):
    if with_filter:
        yr_ref, yi_ref, hr_ref, hi_ref, m2r_ref, m2i_ref, gr_ref, gi_ref, y_ref, ur, ui, ybuf = refs
    else:
        yr_ref, yi_ref, m2r_ref, m2i_ref, gr_ref, gi_ref, y_ref, ur, ui, ybuf = refs
    n2, k1n, j1 = plan.n2, plan.k1, plan.j1

    def stage2(k, c):
        st = pl.multiple_of(k * n2, 16)
        zr = yr_ref[0, pl.ds(st, n2), :]
        zi = yi_ref[0, pl.ds(st, n2), :]
        if with_filter:
            xr, xi = zr.astype(F32), zi.astype(F32)
            hr, hi = hr_ref[0, pl.ds(st, n2), :].astype(F32), hi_ref[0, pl.ds(st, n2), :].astype(F32)
            zr = (xr * hr - xi * hi).astype(BF16)
            zi = (xr * hi + xi * hr).astype(BF16)
        w = (jnp.dot(m2r_ref[...], zr, preferred_element_type=F32)
             + jnp.dot(m2i_ref[...], zi, preferred_element_type=F32))
        ur[pl.ds(st, n2), :] = w[:n2]
        ui[pl.ds(st, n2), :] = w[n2:]
        return c

    lax.fori_loop(0, k1n, stage2, 0, unroll=DFT_UNROLL)

    def stage1(j2, c):
        vr = ur[pl.ds(j2, k1n, stride=n2), :].astype(BF16)
        vi = ui[pl.ds(j2, k1n, stride=n2), :].astype(BF16)
        y = (jnp.dot(gr_ref[j2], vr, preferred_element_type=F32)
             + jnp.dot(gi_ref[j2], vi, preferred_element_type=F32))
        ybuf[pl.ds(j2, j1, stride=n2), :] = y
        return c

    lax.fori_loop(0, n2, stage1, 0, unroll=DFT_UNROLL)
    y_ref[0] = ybuf[pl.ds(0, plan.ns), :]


def _hy_inv_fft(yr, yi, plan, tabs, filt=None):
    q, rows, dh = yr.shape
    cb_ = LANES
    spec_in = pl.BlockSpec((1, rows, cb_), lambda b, c: (b, 0, c))
    filt_specs = [pl.BlockSpec((1, rows, cb_), lambda b, c: (0, 0, c))] * 2 if filt is not None else []
    return pl.pallas_call(
        functools.partial(_hy_inv_fft_kernel, plan=plan, with_filter=filt is not None),
        grid=(q, dh // cb_),
        in_specs=[spec_in, spec_in] + filt_specs
        + [_const_spec(tabs["m2i_r"].shape), _const_spec(tabs["m2i_i"].shape),
           _const_spec(tabs["gi_r"].shape), _const_spec(tabs["gi_i"].shape)],
        out_specs=pl.BlockSpec((1, plan.ns, cb_), lambda b, c: (b, 0, c)),
        out_shape=jax.ShapeDtypeStruct((q, plan.ns, dh), F32),
        scratch_shapes=[pltpu.VMEM((rows, cb_), F32), pltpu.VMEM((rows, cb_), F32),
                        pltpu.VMEM((plan.j1 * plan.n2, cb_), F32)],
        compiler_params=_params(("parallel", "parallel")),
        name="hyena_inv_fft",
    )(yr, yi, *(filt or ()), tabs["m2i_r"], tabs["m2i_i"], tabs["gi_r"], tabs["gi_i"])


def _rope_tables(seq_len):
    half = QK_ROPE // 2
    inv = 1.0 / (ROPE_THETA ** (jnp.arange(0, QK_ROPE, 2, dtype=F32) / QK_ROPE))
    ang = jnp.arange(seq_len, dtype=F32)[:, None] * inv[None, :]
    cos2 = jnp.concatenate([jnp.cos(ang)] * 2, axis=-1)
    sin2 = jnp.concatenate([jnp.sin(ang)] * 2, axis=-1)
    scale = (QK_NOPE + QK_ROPE) ** -0.5 * math.log2(math.e)
    zpad = jnp.zeros((seq_len, HEAD_PAD - QK_NOPE - QK_ROPE), F32)
    ones = jnp.ones((seq_len, QK_NOPE), F32)
    zer = jnp.zeros((seq_len, QK_NOPE), F32)
    zk = jnp.zeros((seq_len, LANES - QK_ROPE), F32)
    del half
    return jnp.concatenate([scale * ones, scale * cos2, zpad, zer, scale * sin2, zpad,
                            cos2, zk, sin2, zk], axis=-1)


def _mla_prep_kernel(pc_ref, tab_ref, gq_ref, gkv_ref, wa_ref, wb_ref, wkn_ref, wv_ref,
                     q_ref, k_ref, v_ref, kmax_ref, *, ql, kvl, seq_len, tb):
    hp = HEAD_PAD
    i = pl.program_id(1)
    live = (i * tb + lax.broadcasted_iota(I32, (tb, 1), 0)) < seq_len
    lane8 = lax.broadcasted_iota(I32, (8, LANES), 1)

    @pl.when(i == 0)
    def _():
        kmax_ref[...] = jnp.zeros_like(kmax_ref)

    pc = pc_ref[0].astype(F32)
    tab = tab_ref[...]
    qc, qs = tab[:, 0:hp], tab[:, hp:2 * hp]
    kc, ks = tab[:, 2 * hp:2 * hp + LANES], tab[:, 2 * hp + LANES:2 * hp + 2 * LANES]

    def rms(x, g):
        return (x * lax.rsqrt(jnp.mean(x * x, axis=-1, keepdims=True) + RMS_EPS) * g).astype(BF16)

    cq = rms(pc[:, 0:ql], gq_ref[...])
    ckv = rms(pc[:, ql:ql + kvl], gkv_ref[...])
    kr = pc[:, ql + kvl:ql + kvl + LANES] * kc + pc[:, ql + kvl + LANES:ql + kvl + 2 * LANES] * ks
    kr = kr.astype(BF16)
    kr_sq = jnp.sum(jnp.square(kr.astype(F32)), axis=-1, keepdims=True)
    lane_t = lax.broadcasted_iota(I32, (tb, hp - V_DIM), 1)
    ones_col = jnp.where(lane_t == 0, 1.0, 0.0).astype(BF16)
    k_tail = jnp.where(lane_t == QK_ROPE, 1.0, kr.astype(F32)).astype(BF16)
    kmax = kmax_ref[0]
    for h in range(N_HEADS):
        qa = jnp.dot(cq, wa_ref[:, h * hp:(h + 1) * hp], preferred_element_type=F32)
        qb = jnp.dot(cq, wb_ref[:, h * hp:(h + 1) * hp], preferred_element_type=F32)
        q_ref[0, :, h * hp:(h + 1) * hp] = (qa * qc + qb * qs).astype(BF16)
        kn = jnp.dot(ckv, wkn_ref[:, h * QK_NOPE:(h + 1) * QK_NOPE], preferred_element_type=F32).astype(BF16)
        k_ref[0, :, h * hp:h * hp + QK_NOPE] = kn
        k_ref[0, :, h * hp + QK_NOPE:(h + 1) * hp] = k_tail
        k_sq = jnp.sum(jnp.square(kn.astype(F32)), axis=-1, keepdims=True) + kr_sq
        k_sq_max = jnp.max(jnp.where(live, k_sq, 0.0), axis=0, keepdims=True)
        kmax = jnp.where(lane8 == h, jnp.maximum(kmax, k_sq_max), kmax)
        vv = jnp.dot(ckv, wv_ref[:, h * V_DIM:(h + 1) * V_DIM], preferred_element_type=F32)
        v_ref[0, :, h * hp:h * hp + V_DIM] = vv.astype(BF16)
        v_ref[0, :, h * hp + V_DIM:(h + 1) * hp] = ones_col
    kmax_ref[0] = kmax


def _mla_prep(pc3, tab, gq, gkv, wa, wb, wkn, wv):
    B, L, pcw = pc3.shape
    ql, kvl = gq.shape[-1], gkv.shape[-1]
    tb = ROW_BLOCK
    hq = N_HEADS * HEAD_PAD
    hv = N_HEADS * HEAD_PAD
    blk = lambda w: pl.BlockSpec((1, tb, w), lambda b, i: (b, i, 0))
    return pl.pallas_call(
        functools.partial(_mla_prep_kernel, ql=ql, kvl=kvl, seq_len=L, tb=tb),
        grid=(B, pl.cdiv(L, tb)),
        in_specs=[blk(pcw), pl.BlockSpec((tb, tab.shape[1]), lambda b, i: (i, 0)),
                  _const_spec((1, ql)), _const_spec((1, kvl)),
                  _const_spec(wa.shape), _const_spec(wb.shape), _const_spec(wkn.shape), _const_spec(wv.shape)],
        out_specs=[blk(hq), blk(hq), blk(hv), pl.BlockSpec((1, 8, LANES), lambda b, i: (b, 0, 0))],
        out_shape=[jax.ShapeDtypeStruct((B, L, hq), BF16), jax.ShapeDtypeStruct((B, L, hq), BF16),
                   jax.ShapeDtypeStruct((B, L, hv), BF16), jax.ShapeDtypeStruct((B, 8, LANES), F32)],
        compiler_params=_params(("parallel", "arbitrary")),
        name="mla_prep",
    )(pc3, tab, gq.reshape(1, ql), gkv.reshape(1, kvl), wa, wb, wkn, wv)


def _attn_kernel(q_ref, k_ref, v_ref, kmax_ref, o_ref, *, seq_len, ck, unroll):
    h = pl.program_id(1)
    i = pl.program_id(2)
    q = q_ref[0]
    tq = q.shape[0]
    nt = (((1,), (1,)), ((), ()))
    full = seq_len // ck
    lead = seq_len % ck

    def over_chunks(step, state, per_trip):
        if lead:
            state = step(0, lead, state)
        for c in range(full % per_trip):
            state = step(lead + c * ck, ck, state)
        base = lead + (full % per_trip) * ck

        def body(it, st):
            for c in range(per_trip):
                st = step(pl.multiple_of(base + (it * per_trip + c) * ck, 16), ck, st)
            return st

        return lax.fori_loop(0, full // per_trip, body, state)

    qf = q.astype(F32)
    q_norm = jnp.sqrt(jnp.sum(qf * qf, axis=-1, keepdims=True))
    lane8 = lax.broadcasted_iota(I32, (8, LANES), 1)
    k_sq_max = jnp.max(jnp.where(lane8 == h, kmax_ref[0], 0.0), axis=(0, 1), keepdims=True)
    bound = q_norm * jnp.sqrt(k_sq_max)
    lane = lax.broadcasted_iota(I32, (tq, HEAD_PAD), 1)
    q_shift = jnp.where(lane == QK_NOPE + QK_ROPE, (-bound).astype(BF16), q)

    def fast(start, size, acc):
        s = lax.dot_general(q_shift, k_ref[0, pl.ds(start, size), :], nt, preferred_element_type=F32)
        p = jnp.exp2(s).astype(BF16)
        return acc + jnp.dot(p, v_ref[0, pl.ds(start, size), :], preferred_element_type=F32)

    acc = over_chunks(fast, jnp.zeros((tq, HEAD_PAD), F32), unroll)
    denom = acc[:, V_DIM:V_DIM + 1]
    o_ref[0] = (acc[:, :V_DIM] / denom).astype(BF16)

    rows = i * tq + lax.broadcasted_iota(I32, (tq, 1), 0)
    healthy = jnp.where(rows < seq_len, denom, 1.0) >= ATT_MIN_DENOM
    n_bad = jnp.sum(jnp.where(healthy, 0.0, 1.0))

    @pl.when(n_bad > 0.0)
    def _():
        def exact(start, size, state):
            m, acc_e = state
            s = lax.dot_general(q, k_ref[0, pl.ds(start, size), :], nt, preferred_element_type=F32)
            m_new = jnp.maximum(m, jnp.max(s, axis=-1, keepdims=True))
            p = jnp.exp2(s - m_new).astype(BF16)
            acc_e = jnp.exp2(m - m_new) * acc_e + jnp.dot(
                p, v_ref[0, pl.ds(start, size), :], preferred_element_type=F32)
            return m_new, acc_e

        _, acc_e = over_chunks(exact, (jnp.full((tq, 1), -1e30, F32), jnp.zeros((tq, HEAD_PAD), F32)),
                               ATT_EXACT_UNROLL)
        o_ref[0] = (acc_e[:, :V_DIM] / acc_e[:, V_DIM:V_DIM + 1]).astype(BF16)


def _attention(q, k, v, kmax):
    B, L, _ = q.shape
    tq = ATT_TQ
    ck = min(ATT_CK, L)
    return pl.pallas_call(
        functools.partial(_attn_kernel, seq_len=L, ck=ck, unroll=ATT_UNROLL),
        grid=(B, N_HEADS, pl.cdiv(L, tq)),
        in_specs=[pl.BlockSpec((1, tq, HEAD_PAD), lambda b, h, i: (b, i, h)),
                  pl.BlockSpec((1, L, HEAD_PAD), lambda b, h, i: (b, 0, h)),
                  pl.BlockSpec((1, L, HEAD_PAD), lambda b, h, i: (b, 0, h)),
                  pl.BlockSpec((1, 8, LANES), lambda b, h, i: (b, 0, 0))],
        out_specs=pl.BlockSpec((1, tq, V_DIM), lambda b, h, i: (b, i, h)),
        out_shape=jax.ShapeDtypeStruct((B, L, N_HEADS * V_DIM), BF16),
        compiler_params=_params(("parallel", "parallel", "arbitrary")),
        name="mla_attention",
    )(q, k, v, kmax)


def _merge_kernel(h2_ref, gb_ref, y_ref, s_ref, x0_ref, skip_ref, o_ref, pg_ref,
                  wpa_ref, wpb_ref, wpc_ref, m_ref, *, d):
    gb = gb_ref[...].astype(F32)
    xa = (jax.nn.gelu(gb) * (h2_ref[0] + h2_ref[1])).astype(BF16)
    xb = (x0_ref[...].astype(F32) * (y_ref[...] + s_ref[...] * skip_ref[...])).astype(BF16)
    ya = jnp.dot(xa, wpa_ref[...], preferred_element_type=F32)
    yb = jnp.dot(xb, wpb_ref[...], preferred_element_type=F32)
    yc = jnp.dot(o_ref[...], wpc_ref[...], preferred_element_type=F32)
    g = jax.nn.sigmoid(pg_ref[...].astype(F32))
    m_ref[...] = (g[:, 0:d] * ya + g[:, d:2 * d] * yb + g[:, 2 * d:3 * d] * yc).astype(BF16)


def _merge(h2, pa, y, s, x0, skip, o, pg, wpa, wpb, wpc):
    T = o.shape[0]
    d = wpa.shape[1]
    dr, dh = wpa.shape[0], wpb.shape[0]
    rb = MERGE_ROW_BLOCK
    row = lambda w, col=0: pl.BlockSpec((rb, w), lambda i: (i, col))
    return pl.pallas_call(
        functools.partial(_merge_kernel, d=d),
        grid=(pl.cdiv(T, rb),),
        in_specs=[pl.BlockSpec((2, rb, dr), lambda i: (0, i, 0)), row(dr, 1), row(dh), row(dh), row(dh),
                  _const_spec((1, dh)), row(o.shape[1]), row(3 * d),
                  _const_spec(wpa.shape), _const_spec(wpb.shape), _const_spec(wpc.shape)],
        out_specs=row(d),
        out_shape=jax.ShapeDtypeStruct((T, d), BF16),
        compiler_params=_params(("parallel",)),
        name="branch_merge",
    )(h2, pa, y, s, x0, skip.reshape(1, dh), o, pg, wpa, wpb, wpc)


def _route(logits):
    ng, ne = N_GROUPS, N_GROUPS * EXPERTS_PER_GROUP
    lane = lax.broadcasted_iota(I32, logits.shape, 1)
    neg = jnp.float32(-jnp.inf)
    big = jnp.int32(4 * LANES)
    gl = jnp.where(lane < ng, logits, neg)
    gmax = jnp.max(gl, axis=-1, keepdims=True)
    g_sel = jnp.min(jnp.where(gl == gmax, lane, big), axis=-1, keepdims=True)
    p_sel = 1.0 / jnp.sum(jnp.exp(gl - gmax), axis=-1, keepdims=True)
    e_lane = lane - ng
    grp_shift = EXPERTS_PER_GROUP.bit_length() - 1
    in_grp = (e_lane >= 0) & (e_lane < ne) & (lax.shift_right_arithmetic(e_lane, grp_shift) == g_sel)
    el = jnp.where(in_grp, logits, neg)
    v0 = jnp.max(el, axis=-1, keepdims=True)
    i0 = jnp.min(jnp.where(el == v0, lane, big), axis=-1, keepdims=True)
    el1 = jnp.where(lane == i0, neg, el)
    v1 = jnp.max(el1, axis=-1, keepdims=True)
    i1 = jnp.min(jnp.where(el1 == v1, lane, big), axis=-1, keepdims=True)
    t = jnp.exp(v1 - v0)
    w0 = p_sel / (1.0 + t)
    w1 = p_sel * t / (1.0 + t)
    e0 = (i0 - ng).astype(F32)
    e1 = (i1 - ng).astype(F32)
    return jnp.where(lane == 0, e0, jnp.where(lane == 1, e1, jnp.where(lane == 2, w0, jnp.where(lane == 3, w1, 0.0))))


def _outproj_kernel(m_ref, h_ref, wout_ref, g_ref, b_ref, wr_hi_ref, wr_lo_ref, br_ref,
                    h1_ref, route_ref, *, alpha):
    mixed = jnp.dot(m_ref[...], wout_ref[...], preferred_element_type=F32)
    h1 = _layer_norm_val(alpha * h_ref[...] + mixed, g_ref[...], b_ref[...])
    h1_ref[...] = h1
    hi = h1.astype(BF16)
    lo = (h1 - hi.astype(F32)).astype(BF16)
    logits = (jnp.dot(hi, wr_hi_ref[...], preferred_element_type=F32)
              + jnp.dot(lo, wr_hi_ref[...], preferred_element_type=F32)
              + jnp.dot(hi, wr_lo_ref[...], preferred_element_type=F32)) + br_ref[...]
    route_ref[...] = _route(logits)


def _outproj(m, h, wout, g, b, wr_hi, wr_lo, br, alpha):
    T, d = h.shape
    rb = ROW_BLOCK
    row = lambda w: pl.BlockSpec((rb, w), lambda i: (i, 0))
    return pl.pallas_call(
        functools.partial(_outproj_kernel, alpha=alpha),
        grid=(pl.cdiv(T, rb),),
        in_specs=[row(d), row(d), _const_spec(wout.shape), _const_spec((1, d)), _const_spec((1, d)),
                  _const_spec(wr_hi.shape), _const_spec(wr_lo.shape), _const_spec((1, LANES))],
        out_specs=[row(d), row(LANES)],
        out_shape=[jax.ShapeDtypeStruct((T, d), F32), jax.ShapeDtypeStruct((T, LANES), F32)],
        compiler_params=_params(("parallel",)),
        name="outproj_ln_router",
    )(m, h, wout, g.reshape(1, d), b.reshape(1, d), wr_hi, wr_lo, br)


def _lane_cumsum(x):
    lane = lax.broadcasted_iota(I32, x.shape, 1)
    s = 1
    while s < x.shape[1]:
        x = x + jnp.where(lane >= s, pltpu.roll(x, s, axis=1), 0.0)
        s *= 2
    return x


def _assign_onehots(route, i, tb, tokens):
    lane = lax.broadcasted_iota(I32, (tb, LANES), 1)
    row = i * tb + lax.broadcasted_iota(I32, (tb, 1), 0)
    live = row < tokens
    e0 = jnp.where(live, route[:, 0:1], -1.0).astype(I32)
    e1 = jnp.where(live, route[:, 1:2], -1.0).astype(I32)
    oh0 = jnp.where(lane == e0, 1.0, 0.0)
    oh1 = jnp.where(lane == e1, 1.0, 0.0)
    return oh0, oh1


def _count_kernel(route_ref, counts_ref, *, tokens, tb):
    i = pl.program_id(0)
    oh0, oh1 = _assign_onehots(route_ref[...], i, tb, tokens)

    @pl.when(i == 0)
    def _():
        counts_ref[...] = jnp.zeros_like(counts_ref)

    counts_ref[0:1, :] += jnp.sum(oh0 + oh1, axis=0, keepdims=True)


def _rank_kernel(route_ref, counts_ref, dest_ref, blk_ref, base, *, tokens, tb, blk, nblocks):
    i = pl.program_id(0)
    ne = N_GROUPS * EXPERTS_PER_GROUP
    lane = lax.broadcasted_iota(I32, (tb, LANES), 1)
    oh0, oh1 = _assign_onehots(route_ref[...], i, tb, tokens)
    both = oh0 + oh1

    @pl.when(i == 0)
    def _():
        lane1 = lax.broadcasted_iota(I32, (1, LANES), 1)
        cnt = jnp.where(lane1 < ne, counts_ref[0:1, :], 0.0)
        padded = jnp.ceil(cnt / blk) * blk
        pend = _lane_cumsum(padded)
        base[...] = pend - padded
        jrow = lax.broadcasted_iota(I32, (nblocks, LANES), 0).astype(F32) * blk
        lanej = lax.broadcasted_iota(I32, (nblocks, LANES), 1)
        ended = jnp.where((lanej < ne) & (pend <= jrow), 1.0, 0.0)
        bexp = jnp.minimum(jnp.sum(ended, axis=-1, keepdims=True), ne - 1.0)
        used = jnp.sum(jnp.where(lane1 == ne - 1, pend, 0.0), axis=-1, keepdims=True) / blk
        blk_ref[...] = jnp.where(lanej == 0, bexp, jnp.where(lanej == 1, used, 0.0))

    r_i = lax.broadcasted_iota(I32, (tb, tb), 0)
    c_i = lax.broadcasted_iota(I32, (tb, tb), 1)
    ltri = jnp.where(c_i < r_i, 1.0, 0.0).astype(BF16)
    prefix = jnp.dot(ltri, both.astype(BF16), preferred_element_type=F32)
    slot = base[...] + prefix
    d0 = jnp.sum(oh0 * slot, axis=-1, keepdims=True)
    d1 = jnp.sum(oh1 * slot, axis=-1, keepdims=True)
    dest_ref[...] = jnp.where(lane == 0, d0, jnp.where(lane == 1, d1, 0.0))
    base[...] += jnp.sum(both, axis=0, keepdims=True)


def _dispatch(route, blk):
    T = route.shape[0]
    tb = ROW_BLOCK
    ne = N_GROUPS * EXPERTS_PER_GROUP
    nblocks = (2 * T + ne * (blk - 1) + blk - 1) // blk
    nbp = _round_up(nblocks, 8)
    rspec = pl.BlockSpec((tb, LANES), lambda i: (i, 0))
    cspec = pl.BlockSpec((8, LANES), lambda i: (0, 0))
    counts = pl.pallas_call(
        functools.partial(_count_kernel, tokens=T, tb=tb),
        grid=(pl.cdiv(T, tb),),
        in_specs=[rspec],
        out_specs=cspec,
        out_shape=jax.ShapeDtypeStruct((8, LANES), F32),
        compiler_params=_params(("arbitrary",)),
        name="moe_counts",
    )(route)
    dest, blkinfo = pl.pallas_call(
        functools.partial(_rank_kernel, tokens=T, tb=tb, blk=blk, nblocks=nbp),
        grid=(pl.cdiv(T, tb),),
        in_specs=[rspec, cspec],
        out_specs=[rspec, pl.BlockSpec((nbp, LANES), lambda i: (0, 0))],
        out_shape=[jax.ShapeDtypeStruct((T, LANES), F32), jax.ShapeDtypeStruct((nbp, LANES), F32)],
        scratch_shapes=[pltpu.VMEM((1, LANES), F32)],
        compiler_params=_params(("arbitrary",)),
        name="moe_rank",
    )(route, counts)
    return dest, blkinfo, nblocks


def _for_row_groups(n_rows, fn):
    def body(g, c):
        for j in range(DMA_GROUP):
            fn(g * DMA_GROUP + j)
        return c

    lax.fori_loop(0, n_rows // DMA_GROUP, body, 0)


def _scatter_kernel(dest_ref, x_ref, xs_in, xs_out, sem, *, tb, tokens):
    del xs_in
    live_rows = jnp.minimum(tb, tokens - pl.program_id(0) * tb)

    def copy(r, s):
        d = dest_ref[0, 0, 2 * r + s]
        return pltpu.make_async_copy(x_ref.at[pl.ds(r, 1)], xs_out.at[pl.ds(d, 1)], sem)

    def start(r):
        copy(r, 0).start()
        copy(r, 1).start()

    def wait(r):
        copy(r, 0).wait()
        copy(r, 1).wait()

    _for_row_groups(live_rows, start)
    _for_row_groups(live_rows, wait)


def _moe_scatter(h1, dest_idx, rows, tb):
    T, d = h1.shape
    assert T % DMA_GROUP == 0
    nblk = dest_idx.shape[0]
    xs0 = jnp.zeros((rows, d), F32)
    return pl.pallas_call(
        functools.partial(_scatter_kernel, tb=tb, tokens=T),
        grid=(nblk,),
        in_specs=[pl.BlockSpec((1, 1, 2 * tb), lambda i: (i, 0, 0), memory_space=pltpu.SMEM),
                  pl.BlockSpec((tb, d), lambda i: (i, 0)),
                  pl.BlockSpec(memory_space=pl.ANY)],
        out_specs=pl.BlockSpec(memory_space=pl.ANY),
        out_shape=jax.ShapeDtypeStruct((rows, d), F32),
        scratch_shapes=[pltpu.SemaphoreType.DMA(())],
        input_output_aliases={2: 0},
        compiler_params=_params(("arbitrary",)),
        name="moe_scatter",
    )(dest_idx, h1, xs0)


def _expert_kernel(be_ref, nu_ref, ly_ref, x_ref, wg_ref, wu_ref, wd_ref, y_ref):
    del be_ref, ly_ref
    i = pl.program_id(0)

    @pl.when(i < nu_ref[0])
    def _():
        x = x_ref[...].astype(BF16)
        g = jnp.dot(x, wg_ref[0, 0], preferred_element_type=F32)
        u = jnp.dot(x, wu_ref[0, 0], preferred_element_type=F32)
        mid = (g * jax.nn.sigmoid(g) * u).astype(BF16)
        y_ref[...] = jnp.dot(mid, wd_ref[0, 0], preferred_element_type=F32)

    @pl.when(i >= nu_ref[0])
    def _():
        y_ref[...] = jnp.zeros_like(y_ref)


def _moe_experts(xs, blk_e, n_used, layer, wg, wu, wd, blk, nblocks):
    d = xs.shape[1]
    de = wg.shape[-1]
    grid_spec = pltpu.PrefetchScalarGridSpec(
        num_scalar_prefetch=3,
        grid=(nblocks,),
        in_specs=[pl.BlockSpec((blk, d), lambda i, be, nu, ly: (i, 0)),
                  pl.BlockSpec((1, 1, d, de), lambda i, be, nu, ly: (ly[0], be[i], 0, 0)),
                  pl.BlockSpec((1, 1, d, de), lambda i, be, nu, ly: (ly[0], be[i], 0, 0)),
                  pl.BlockSpec((1, 1, de, d), lambda i, be, nu, ly: (ly[0], be[i], 0, 0))],
        out_specs=pl.BlockSpec((blk, d), lambda i, be, nu, ly: (i, 0)),
    )
    return pl.pallas_call(
        _expert_kernel,
        grid_spec=grid_spec,
        out_shape=jax.ShapeDtypeStruct((nblocks * blk, d), F32),
        compiler_params=_params(("arbitrary",)),
        name="moe_experts",
    )(blk_e, n_used, layer, xs, wg, wu, wd)


def _combine_kernel(dest_ref, h_ref, route_ref, g_ref, b_ref, ys_hbm, o_ref, ob_ref, gbuf, sem, *, tb, alpha):
    def copy(r, s):
        d = dest_ref[0, 0, 2 * r + s]
        return pltpu.make_async_copy(ys_hbm.at[pl.ds(d, 1)], gbuf.at[s, pl.ds(r, 1)], sem)

    def start(r):
        copy(r, 0).start()
        copy(r, 1).start()

    def wait(r):
        copy(r, 0).wait()
        copy(r, 1).wait()

    _for_row_groups(tb, start)
    _for_row_groups(tb, wait)
    route = route_ref[...]
    moe = route[:, 2:3] * gbuf[0] + route[:, 3:4] * gbuf[1]
    y = _layer_norm_val(alpha * h_ref[...] + moe, g_ref[...], b_ref[...])
    o_ref[...] = y
    ob_ref[...] = y.astype(BF16)


def _moe_combine(h1, route, dest_idx, ys, g, b, alpha, tb):
    T, d = h1.shape
    nblk = dest_idx.shape[0]
    row = lambda w: pl.BlockSpec((tb, w), lambda i: (i, 0))
    return pl.pallas_call(
        functools.partial(_combine_kernel, tb=tb, alpha=alpha),
        grid=(nblk,),
        in_specs=[pl.BlockSpec((1, 1, 2 * tb), lambda i: (i, 0, 0), memory_space=pltpu.SMEM),
                  row(d), row(LANES), _const_spec((1, d)), _const_spec((1, d)),
                  pl.BlockSpec(memory_space=pl.ANY)],
        out_specs=[row(d), row(d)],
        out_shape=[jax.ShapeDtypeStruct((T, d), F32), jax.ShapeDtypeStruct((T, d), BF16)],
        scratch_shapes=[pltpu.VMEM((2, tb, d), F32), pltpu.SemaphoreType.DMA(())],
        compiler_params=_params(("arbitrary",)),
        name="moe_combine_ln",
    )(dest_idx, h1, route, g.reshape(1, d), b.reshape(1, d), ys)


def _moe(h1, route, layer, wg, wu, wd, g, b, alpha):
    T, d = h1.shape
    blk = MOE_BLOCK
    tb = MOE_TOKEN_BLOCK
    dest, blkinfo, nblocks = _dispatch(route, blk)
    nblk_tok = pl.cdiv(T, tb)
    di = dest[:, 0:2].astype(I32)
    dest_idx = jnp.pad(di, ((0, nblk_tok * tb - T), (0, 0))).reshape(nblk_tok, 1, 2 * tb)
    blk_e = blkinfo[:nblocks, 0].astype(I32)
    n_used = blkinfo[0:1, 1].astype(I32)
    xs = _moe_scatter(h1, dest_idx, nblocks * blk, tb)
    ys = _moe_experts(xs, blk_e, n_used, layer, wg, wu, wd, blk, nblocks)
    return _moe_combine(h1, route, dest_idx, ys, g, b, alpha, tb)


def _prep_weights(W):
    depth, d, _ = W["w_in"].shape
    dr = W["conv_a_w"].shape[-1]
    dh = W["conv_b_w"].shape[-1] // 3
    ql, kvl = W["q_norm_g"].shape[-1], W["kv_norm_g"].shape[-1]
    ca, cbw = 2 * dr, 3 * dh
    cc = ql + kvl + QK_ROPE
    w_in = W["w_in"]
    half = QK_ROPE // 2

    def rot(w):
        return jnp.concatenate([-w[..., half:], w[..., :half]], axis=-1)

    w_a = w_in[..., :ca]
    w_b = w_in[..., ca:ca + cbw]
    w_c = w_in[..., ca + cbw:ca + cbw + cc]
    w_g = w_in[..., ca + cbw + cc:]
    w_kr = w_c[..., ql + kvl:]
    zpad = jnp.zeros(w_kr.shape[:-1] + (LANES - QK_ROPE,), F32)
    w_c_ext = jnp.concatenate([w_c[..., :ql + kvl], w_kr, zpad, rot(w_kr), zpad], axis=-1)

    wq = W["w_uq"].reshape(depth, ql, N_HEADS, QK_NOPE + QK_ROPE)
    z_n = jnp.zeros((depth, ql, N_HEADS, QK_NOPE), F32)
    z_p = jnp.zeros((depth, ql, N_HEADS, HEAD_PAD - QK_NOPE - QK_ROPE), F32)
    wa = jnp.concatenate([wq, z_p], axis=-1).reshape(depth, ql, N_HEADS * HEAD_PAD)
    wb = jnp.concatenate([z_n, rot(wq[..., QK_NOPE:]), z_p], axis=-1).reshape(depth, ql, N_HEADS * HEAD_PAD)
    wkv = W["w_ukv"].reshape(depth, kvl, N_HEADS, QK_NOPE + V_DIM)
    wkn = wkv[..., :QK_NOPE].reshape(depth, kvl, N_HEADS * QK_NOPE)
    wv = wkv[..., QK_NOPE:].reshape(depth, kvl, N_HEADS * V_DIM)

    ng, ne = N_GROUPS, N_GROUPS * EXPERTS_PER_GROUP
    wr = jnp.concatenate([W["w_rg"], W["w_re"], jnp.zeros((depth, d, LANES - ng - ne), F32)], axis=-1)
    br = jnp.concatenate([W["b_rg"], W["b_re"], jnp.zeros((depth, LANES - ng - ne), F32)], axis=-1)
    wr_hi = wr.astype(BF16)
    wr_lo = (wr - wr_hi.astype(F32)).astype(BF16)
    bf = lambda x: x.astype(BF16)
    return dict(
        w_a=bf(w_a), w_b=bf(w_b), w_c=bf(w_c_ext), w_g=bf(w_g),
        conv_a_w=W["conv_a_w"], conv_a_b=W["conv_a_b"], rg_wa=bf(W["rg_wa"]), rg_ba=W["rg_ba"],
        rg_wx=bf(W["rg_wx"]), rg_bx=W["rg_bx"], rg_lambda=W["rg_lambda"],
        conv_b_w=W["conv_b_w"], conv_b_b=W["conv_b_b"], hy_w1=W["hy_w1"], hy_b1=W["hy_b1"],
        hy_w2=W["hy_w2"], hy_b2=W["hy_b2"], hy_w3=W["hy_w3"], hy_freq=W["hy_freq"], hy_skip=W["hy_skip"],
        q_norm_g=W["q_norm_g"], kv_norm_g=W["kv_norm_g"], wa=bf(wa), wb=bf(wb), wkn=bf(wkn), wv=bf(wv),
        w_pa=bf(W["w_pa"]), w_pb=bf(W["w_pb"]), w_pc=bf(W["w_pc"]), w_out=bf(W["w_out"]),
        ln1_g=W["ln1_g"], ln1_b=W["ln1_b"], wr_hi=wr_hi, wr_lo=wr_lo, br=br.reshape(depth, 1, LANES),
        w_e_gate=bf(W["w_e_gate"]), w_e_up=bf(W["w_e_up"]), w_e_down=bf(W["w_e_down"]),
        ln2_g=W["ln2_g"], ln2_b=W["ln2_b"],
    )


def _encoder_layer(h, hb, P, experts, B, L, plan, tabs, rope_tab, alpha):
    T = B * L
    pa = _matmul(hb, P["w_a"], BF16, "in_proj_a")
    pb = _matmul(hb, P["w_b"], BF16, "in_proj_b")
    pc = _matmul(hb, P["w_c"], BF16, "in_proj_c")
    pg = _matmul(hb, P["w_g"], BF16, "in_proj_g")

    a, u = _rglru_gates(pa.reshape(B, L, -1), P["conv_a_w"], P["conv_a_b"], P["rg_wa"], P["rg_ba"],
                        P["rg_wx"], P["rg_bx"], P["rg_lambda"])
    h2 = _rglru_scan(a, u)

    s, x0 = _hy_conv3(pb.reshape(B, L, -1), P["conv_b_w"], P["conv_b_b"], plan)
    dh = s.shape[-1]
    hl, norm = _hy_taps(L, plan, P["hy_w1"], P["hy_b1"], P["hy_w2"], P["hy_b2"], P["hy_w3"], P["hy_freq"])
    f_r, f_i = _hy_filter_fft(hl, norm, plan, tabs)
    x_r, x_i = _hy_fwd_fft(s.reshape(B * plan.nb, plan.ns, dh), plan, tabs)
    if plan.nb == 1:
        y = _hy_inv_fft(x_r, x_i, plan, tabs, filt=(f_r, f_i))
    else:
        y_r, y_i = _hy_mac(x_r, x_i, f_r, f_i, B, plan)
        y = _hy_inv_fft(y_r, y_i, plan, tabs)
    y = y.reshape(B, plan.nb * plan.ns, dh)
    if plan.nb * plan.ns != L and B > 1:
        y, s, x0 = y[:, :L], s[:, :L], x0[:, :L]
    flat = lambda t: t.reshape(-1, dh)

    q, k, v, kmax = _mla_prep(pc.reshape(B, L, -1), rope_tab, P["q_norm_g"], P["kv_norm_g"],
                        P["wa"], P["wb"], P["wkn"], P["wv"])
    o = _attention(q, k, v, kmax).reshape(T, -1)

    m = _merge(h2, pa, flat(y), flat(s), flat(x0), P["hy_skip"], o, pg, P["w_pa"], P["w_pb"], P["w_pc"])
    h1, route = _outproj(m, h, P["w_out"], P["ln1_g"], P["ln1_b"], P["wr_hi"], P["wr_lo"], P["br"], alpha)
    return _moe(h1, route, P["layer"], experts["w_e_gate"], experts["w_e_up"], experts["w_e_down"],
                P["ln2_g"], P["ln2_b"], alpha)


def _encoder_trunk(x, meta, ln_g, ln_b, PW, alpha):
    B, S, D = x.shape
    L = S + N_META
    xc = jnp.concatenate([jnp.broadcast_to(meta[None], (B, N_META, D)), x], axis=1).reshape(B * L, D)
    h, hb = _layer_norm_rows(xc, ln_g, ln_b)
    plan = _FftPlan(L)
    tabs = plan.tables()
    rope_tab = _rope_tables(L)
    expert_names = ("w_e_gate", "w_e_up", "w_e_down")
    experts = {k: PW[k] for k in expert_names}
    per_layer = {k: v for k, v in PW.items() if k not in expert_names}
    per_layer["layer"] = jnp.arange(PW["w_a"].shape[0], dtype=I32).reshape(-1, 1)

    def body(carry, P):
        return _encoder_layer(carry[0], carry[1], P, experts, B, L, plan, tabs, rope_tab, alpha), None

    (h, hb), _ = lax.scan(body, (h, hb), per_layer)
    return h.reshape(B, L, D)[:, N_META:]


def kernel(x_prompt, x_sample, meta, ln_emb_g, ln_emb_b, w_in, conv_a_w, conv_a_b, rg_wa, rg_ba, rg_wx, rg_bx, rg_lambda, conv_b_w, conv_b_b, hy_w1, hy_b1, hy_w2, hy_b2, hy_w3, hy_freq, hy_skip, q_norm_g, w_uq, kv_norm_g, w_ukv, w_pa, w_pb, w_pc, w_out, ln1_g, ln1_b, w_rg, b_rg, w_re, b_re, w_e_gate, w_e_up, w_e_down, ln2_g, ln2_b):
    W = dict(w_in=w_in, conv_a_w=conv_a_w, conv_a_b=conv_a_b, rg_wa=rg_wa, rg_ba=rg_ba, rg_wx=rg_wx,
             rg_bx=rg_bx, rg_lambda=rg_lambda, conv_b_w=conv_b_w, conv_b_b=conv_b_b, hy_w1=hy_w1,
             hy_b1=hy_b1, hy_w2=hy_w2, hy_b2=hy_b2, hy_w3=hy_w3, hy_freq=hy_freq, hy_skip=hy_skip,
             q_norm_g=q_norm_g, w_uq=w_uq, kv_norm_g=kv_norm_g, w_ukv=w_ukv, w_pa=w_pa, w_pb=w_pb,
             w_pc=w_pc, w_out=w_out, ln1_g=ln1_g, ln1_b=ln1_b, w_rg=w_rg, b_rg=b_rg, w_re=w_re, b_re=b_re,
             w_e_gate=w_e_gate, w_e_up=w_e_up, w_e_down=w_e_down, ln2_g=ln2_g, ln2_b=ln2_b)
    depth = w_in.shape[0]
    alpha = (2 * depth) ** 0.25
    PW = _prep_weights(W)
    y_prompt = _encoder_trunk(x_prompt, meta, ln_emb_g, ln_emb_b, PW, alpha)
    y_sample = _encoder_trunk(x_sample, meta, ln_emb_g, ln_emb_b, PW, alpha)
    return (y_prompt, y_sample)
```

```python
import functools
import math

import numpy as np
import jax
import jax.numpy as jnp
from jax import lax
from jax.experimental import pallas as pl
from jax.experimental.pallas import tpu as pltpu

F32 = jnp.float32
BF16 = jnp.bfloat16
I32 = jnp.int32

N_META = 16
RG_C = 8.0
HY_BANDS = 16
HY_FAST_DECAY = 0.3
HY_SLOW_DECAY = 1.5
HY_TARGET = 1e-2
N_HEADS = 16
QK_NOPE = 128
QK_ROPE = 64
V_DIM = 128
ROPE_THETA = 10000.0
N_GROUPS = 8
EXPERTS_PER_GROUP = 8
LN_EPS = 1e-5
RMS_EPS = 1e-6

V7X_VMEM_BYTES = 64 * 1024 * 1024
VMEM_LIMIT = 56 * 1024 * 1024
LANES = 128
HEAD_PAD = 2 * LANES

ROW_BLOCK = 512
MM_ROW_BLOCK = 1024
MM_SINGLE_TILE_COLS = 1536
HALO = 16
HY_MAX_BLOCK = 4608
MOE_BLOCK = 256
MOE_TOKEN_BLOCK = 256
MERGE_ROW_BLOCK = 256
DMA_GROUP = 8
ATT_TQ = 384
ATT_CK = 512
ATT_UNROLL = 32
ATT_EXACT_UNROLL = 2
ATT_MIN_DENOM = 2.0 ** -60
DFT_UNROLL = 8
SCAN_BATCH_GROUP = 4
SCAN_BLOCK_BYTES = 9 * 512 * 1024


def _params(sem, vmem=VMEM_LIMIT):
    return pltpu.CompilerParams(dimension_semantics=sem, vmem_limit_bytes=vmem)


def _round_up(x, m):
    return (x + m - 1) // m * m


def _pick_tile(n, cands):
    for c in cands:
        if n % c == 0:
            return c
    return n


def _const_spec(shape):
    nd = len(shape)
    return pl.BlockSpec(shape, lambda *a: (0,) * nd, pipeline_mode=pl.Buffered(1))


def _layer_norm_val(x, g, b):
    mu = jnp.mean(x, axis=-1, keepdims=True)
    xc = x - mu
    var = jnp.mean(xc * xc, axis=-1, keepdims=True)
    return xc * lax.rsqrt(var + LN_EPS) * g + b


def _ln_kernel(x_ref, g_ref, b_ref, o_ref, ob_ref):
    y = _layer_norm_val(x_ref[...], g_ref[...], b_ref[...])
    o_ref[...] = y
    ob_ref[...] = y.astype(BF16)


def _layer_norm_rows(x, g, b):
    T, D = x.shape
    rb = ROW_BLOCK
    row = pl.BlockSpec((rb, D), lambda i: (i, 0))
    vec = pl.BlockSpec((1, D), lambda i: (0, 0))
    return pl.pallas_call(
        _ln_kernel,
        grid=(pl.cdiv(T, rb),),
        in_specs=[row, vec, vec],
        out_specs=[row, row],
        out_shape=[jax.ShapeDtypeStruct((T, D), F32), jax.ShapeDtypeStruct((T, D), BF16)],
        compiler_params=_params(("parallel",)),
        name="embed_ln",
    )(x, g.reshape(1, D), b.reshape(1, D))


def _mm_kernel(x_ref, w_ref, o_ref):
    o_ref[...] = jnp.dot(x_ref[...], w_ref[...], preferred_element_type=F32).astype(o_ref.dtype)


def _matmul(x, w, out_dtype, name):
    T, K = x.shape
    N = w.shape[1]
    rb = MM_ROW_BLOCK
    tn = N if N <= MM_SINGLE_TILE_COLS else _pick_tile(N, (1024, 768, 512, 256, 128))
    return pl.pallas_call(
        _mm_kernel,
        grid=(pl.cdiv(T, rb), N // tn),
        in_specs=[pl.BlockSpec((rb, K), lambda i, j: (i, 0)),
                  pl.BlockSpec((K, tn), lambda i, j: (0, j))],
        out_specs=pl.BlockSpec((rb, tn), lambda i, j: (i, j)),
        out_shape=jax.ShapeDtypeStruct((T, N), out_dtype),
        compiler_params=_params(("parallel", "arbitrary")),
        name=name,
    )(x, w)


def _halo_specs(tb, cb, n_in_blocks, n16, col_of):
    r = tb // HALO
    main = pl.BlockSpec((1, tb, cb), lambda b, i, c: (b, jnp.minimum(i, n_in_blocks - 1), col_of(c)))
    prev = pl.BlockSpec((1, HALO, cb), lambda b, i, c: (b, jnp.clip(i * r - 1, 0, n16 - 1), col_of(c)))
    nxt = pl.BlockSpec((1, HALO, cb), lambda b, i, c: (b, jnp.clip((i + 1) * r, 0, n16 - 1), col_of(c)))
    return [main, prev, nxt]


def _halo_window(xm, xp, xn, i, tb, seq_len):
    w = jnp.concatenate([xp[0], xm[0], xn[0]], axis=0).astype(F32)
    t = i * tb - HALO + lax.broadcasted_iota(I32, (tb + 2 * HALO, 1), 0)
    return jnp.where((t >= 0) & (t < seq_len), w, 0.0)


def _rglru_gate_kernel(xm, xp, xn, cw, cb, wa, ba, wx, bx, lam, a_ref, u_ref, *, tb, seq_len):
    i = pl.program_id(1)
    w = _halo_window(xm, xp, xn, i, tb, seq_len)
    o = HALO
    xb = (cw[0:1] * w[o - 1:o - 1 + tb] + cw[1:2] * w[o:o + tb]
          + cw[2:3] * w[o + 1:o + 1 + tb] + cw[3:4] * w[o + 2:o + 2 + tb] + cb[...])
    for k in range(xb.shape[1] // LANES):
        cols = slice(k * LANES, (k + 1) * LANES)
        xk = xb[:, cols]
        xkb = xk.astype(BF16)
        for d in range(2):
            r = jax.nn.sigmoid(jnp.dot(xkb, wa[d, k], preferred_element_type=F32) + ba[d:d + 1, cols])
            ig = jax.nn.sigmoid(jnp.dot(xkb, wx[d, k], preferred_element_type=F32) + bx[d:d + 1, cols])
            z = -lam[d:d + 1, cols]
            softplus = jnp.maximum(z, 0.0) + jnp.log(1.0 + jnp.exp(-jnp.abs(z)))
            log_a = -RG_C * r * softplus
            a_ref[d, 0, :, cols] = jnp.exp(log_a)
            u_ref[d, 0, :, cols] = jnp.sqrt(1.0 - jnp.exp(2.0 * log_a)) * (ig * xk)


def _rglru_gates(pa3, conv_w, conv_b, wa, ba, wx, bx, lam):
    B, L, _ = pa3.shape
    DR = conv_w.shape[1]
    tb = ROW_BLOCK
    nt = pl.cdiv(L, tb)
    n16 = L // HALO
    vec = lambda rows: pl.BlockSpec((rows, DR), lambda b, i, k: (0, 0))
    wspec = pl.BlockSpec(wa.shape, lambda b, i, k: (0, 0, 0, 0))
    out = pl.BlockSpec((2, 1, tb, DR), lambda b, i, k: (0, b, i, 0))
    return pl.pallas_call(
        functools.partial(_rglru_gate_kernel, tb=tb, seq_len=L),
        grid=(B, nt, 1),
        in_specs=_halo_specs(tb, DR, nt, n16, lambda k: 0)
        + [vec(4), vec(1), wspec, vec(2), wspec, vec(2), vec(2)],
        out_specs=[out, out],
        out_shape=[jax.ShapeDtypeStruct((2, B, L, DR), F32)] * 2,
        compiler_params=_params(("parallel", "parallel", "arbitrary")),
        name="rglru_gates",
    )(pa3, pa3, pa3, conv_w, conv_b.reshape(1, DR), wa, ba, wx, bx, lam)


def _scan_kernel(a_ref, u_ref, o_ref, h_ref, *, steps):
    d = pl.program_id(0)
    c = pl.program_id(2)

    @pl.when(c == 0)
    def _():
        h_ref[...] = jnp.zeros_like(h_ref)

    def body(i, h):
        t = i + d * (steps - 1 - 2 * i)
        h = a_ref[0, :, t] * h + u_ref[0, :, t]
        o_ref[0, :, t] = h
        return h

    h_ref[...] = lax.fori_loop(0, steps, body, h_ref[...], unroll=4)


def _rglru_scan(a, u):
    _, B, L, DR = a.shape
    r = DR // LANES
    a5 = a.reshape(2, B, L, r, LANES)
    u5 = u.reshape(2, B, L, r, LANES)
    bg = SCAN_BATCH_GROUP if B % SCAN_BATCH_GROUP == 0 else B
    budget = SCAN_BLOCK_BYTES // (bg * r * LANES * 4)
    steps = max(s for s in range(1, L + 1) if L % s == 0 and s <= max(budget, 1))
    nch = L // steps
    spec = pl.BlockSpec((1, bg, steps, r, LANES), lambda d, g, c: (d, g, c + d * (nch - 1 - 2 * c), 0, 0))
    h = pl.pallas_call(
        functools.partial(_scan_kernel, steps=steps),
        grid=(2, B // bg, nch),
        in_specs=[spec, spec],
        out_specs=spec,
        out_shape=jax.ShapeDtypeStruct((2, B, L, r, LANES), F32),
        scratch_shapes=[pltpu.VMEM((bg, r, LANES), F32)],
        compiler_params=_params(("arbitrary", "arbitrary", "arbitrary")),
        name="rglru_scan",
    )(a5, u5)
    return h.reshape(2, B * L, DR)


class _FftPlan:
    def __init__(self, seq_len):
        self.nb = -(-seq_len // HY_MAX_BLOCK)
        self.ns = _round_up(-(-seq_len // self.nb), 16)
        best = None
        for n1 in (30, 62, 94, 126):
            n2 = _round_up(-(-(2 * self.ns - 1) // n1), 16)
            cost = n1 * n2 * (n1 + n2)
            if best is None or cost < best[0]:
                best = (cost, n1, n2)
        _, self.n1, self.n2 = best
        self.n = self.n1 * self.n2
        self.k1 = self.n1 // 2 + 1
        self.j1 = _round_up(-(-self.ns // self.n2), 16)
        self.j1f = _round_up(self.n1, 16)
        self.rows = self.k1 * self.n2

    def tables(self):
        n1, n2, n, k1n = self.n1, self.n2, self.n, self.k1
        k1 = np.arange(k1n, dtype=np.float64)[None, :, None]
        j2 = np.arange(n2, dtype=np.float64)[:, None, None]

        def fwd(jcols):
            j1 = np.arange(jcols, dtype=np.float64)[None, None, :]
            ang = 2.0 * np.pi * k1 * (j1 * n2 + j2) / n
            live = (j1 < n1)
            return np.concatenate([np.cos(ang) * live, -np.sin(ang) * live], axis=1)

        g_sig = fwd(self.j1)
        g_filt = fwd(self.j1f)
        a = np.arange(n2, dtype=np.float64)
        phi = 2.0 * np.pi * np.outer(a, a) / n2
        c, s = np.cos(phi), np.sin(phi)
        m2f_r = np.concatenate([c, -s], axis=0)
        m2f_i = np.concatenate([s, c], axis=0)
        m2i_r = np.concatenate([c, s], axis=0)
        m2i_i = np.concatenate([-s, c], axis=0)
        j1 = np.arange(self.j1, dtype=np.float64)[None, :, None]
        kk = np.arange(k1n, dtype=np.float64)[None, None, :]
        ang = 2.0 * np.pi * kk * (j1 * n2 + j2) / n
        coef = np.where((kk == 0) | (kk == n1 // 2), 1.0, 2.0) / n
        gi_r = coef * np.cos(ang)
        gi_i = -coef * np.sin(ang)
        bf = lambda x: jnp.asarray(x, dtype=F32).astype(BF16)
        return dict(g_sig=bf(g_sig), g_filt=bf(g_filt), m2f_r=bf(m2f_r), m2f_i=bf(m2f_i),
                    m2i_r=bf(m2i_r), m2i_i=bf(m2i_i), gi_r=bf(gi_r), gi_i=bf(gi_i))


def _hy_conv3_kernel(*refs, tb, seq_len):
    x_refs = refs[0:9]
    cw = refs[9:12]
    cb = refs[12:15]
    s_ref, x0_ref = refs[15], refs[16]
    i = pl.program_id(1)
    o = HALO
    outs = []
    for p in range(3):
        w = _halo_window(x_refs[3 * p], x_refs[3 * p + 1], x_refs[3 * p + 2], i, tb, seq_len)
        outs.append(cw[p][0:1] * w[o - 1:o - 1 + tb] + cw[p][1:2] * w[o:o + tb]
                    + cw[p][2:3] * w[o + 1:o + 1 + tb] + cb[p][...])
    t = i * tb + lax.broadcasted_iota(I32, (tb, 1), 0)
    live = t < seq_len
    s_ref[0] = jnp.where(live, outs[1] * outs[2], 0.0)
    x0_ref[0] = jnp.where(live, outs[0], 0.0).astype(BF16)


def _hy_conv3(pb3, conv_w, conv_b, plan):
    B, L, _ = pb3.shape
    DH = conv_w.shape[1] // 3
    cb_ = _pick_tile(DH, (1024, 512, 256, 128))
    nc = DH // cb_
    tb = ROW_BLOCK
    lp = plan.nb * plan.ns
    nt_in = pl.cdiv(L, tb)
    n16 = L // HALO
    in_specs = []
    for p in range(3):
        in_specs += _halo_specs(tb, cb_, nt_in, n16, lambda c, p=p: p * nc + c)
    in_specs += [pl.BlockSpec((3, cb_), lambda b, i, c, p=p: (0, p * nc + c)) for p in range(3)]
    in_specs += [pl.BlockSpec((1, cb_), lambda b, i, c, p=p: (0, p * nc + c)) for p in range(3)]
    out = pl.BlockSpec((1, tb, cb_), lambda b, i, c: (b, i, c))
    return pl.pallas_call(
        functools.partial(_hy_conv3_kernel, tb=tb, seq_len=L),
        grid=(B, pl.cdiv(lp, tb), nc),
        in_specs=in_specs,
        out_specs=[out, out],
        out_shape=[jax.ShapeDtypeStruct((B, lp, DH), F32), jax.ShapeDtypeStruct((B, lp, DH), BF16)],
        compiler_params=_params(("parallel", "parallel", "parallel")),
        name="hyena_conv3",
    )(*([pb3] * 9), *([conv_w] * 3), *([conv_b.reshape(1, 3 * DH)] * 3))


def _hy_lag_table(seq_len, plan):
    half = plan.nb * plan.ns
    rows = _round_up(2 * half, ROW_BLOCK)
    lag = np.arange(rows) - half
    pos = np.abs(lag)
    valid = (np.arange(rows) < 2 * half) & (pos <= seq_len - 1)
    posc = np.minimum(pos, seq_len - 1)
    t01 = np.linspace(0.0, 1.0, seq_len, dtype=np.float32)
    ang = ((2.0 * math.pi / seq_len) * np.arange(seq_len, dtype=np.float32)).astype(np.float32)
    bands = np.linspace(1e-4, HY_BANDS - 1, HY_BANDS, dtype=np.float32)
    arg = (bands[None, :] * ang[:, None]).astype(np.float64)
    emb = np.concatenate([t01[:, None].astype(np.float64), np.cos(arg), -np.sin(arg)], axis=-1)
    table = np.zeros((rows, LANES), np.float32)
    ne = emb.shape[1]
    table[:, :ne] = emb[posc] * valid[:, None]
    table[:, ne] = valid
    table[:, ne + 1] = lag >= 0
    return jnp.asarray(table), ne


def _hy_taps_kernel(e_ref, w1, b1, w2, b2, w3, fr, rates, hl_ref, norm_ref, *, ne, dh):
    hi = lax.Precision.HIGHEST
    e = e_ref[...]
    f = fr[...]
    z = jnp.sin(f * (jnp.dot(e, w1[...], preferred_element_type=F32, precision=hi) + b1[...]))
    z = jnp.sin(f * (jnp.dot(z, w2[...], preferred_element_type=F32, precision=hi) + b2[...]))
    z3 = jnp.dot(z, w3[...], preferred_element_type=F32, precision=hi)
    t01 = e[:, 0:1]
    valid = e[:, ne:ne + 1]
    is_fwd = e[:, ne + 1:ne + 2]
    decay = jnp.exp(-t01 * rates[...])
    taps = jnp.where(is_fwd > 0.5, z3[:, :dh], z3[:, dh:]) * decay * valid
    hl_ref[...] = taps

    @pl.when(pl.program_id(0) == 0)
    def _():
        norm_ref[...] = jnp.zeros_like(norm_ref)

    norm_ref[0:1, :] += jnp.sum(jnp.abs(taps), axis=0, keepdims=True)


def _hy_taps(seq_len, plan, w1, b1, w2, b2, w3, freq):
    table, ne = _hy_lag_table(seq_len, plan)
    rows = table.shape[0]
    nf = w1.shape[1]
    dh = w3.shape[1] // 2
    w1p = jnp.zeros((LANES, nf), F32).at[:ne].set(w1)
    rates = np.abs(np.linspace(math.log(HY_TARGET) / HY_SLOW_DECAY, math.log(HY_TARGET) / HY_FAST_DECAY,
                               dh, dtype=np.float32)).reshape(1, dh)
    rb = ROW_BLOCK
    full = lambda a: pl.BlockSpec(a.shape, lambda i: (0,) * a.ndim)
    args = (table, w1p, b1.reshape(1, nf), w2, b2.reshape(1, nf), w3, freq.reshape(1, nf), jnp.asarray(rates))
    return pl.pallas_call(
        functools.partial(_hy_taps_kernel, ne=ne, dh=dh),
        grid=(rows // rb,),
        in_specs=[pl.BlockSpec((rb, LANES), lambda i: (i, 0))] + [full(a) for a in args[1:]],
        out_specs=[pl.BlockSpec((rb, dh), lambda i: (i, 0)), pl.BlockSpec((8, dh), lambda i: (0, 0))],
        out_shape=[jax.ShapeDtypeStruct((rows, dh), F32), jax.ShapeDtypeStruct((8, dh), F32)],
        compiler_params=_params(("arbitrary",)),
        name="hyena_taps",
    )(*args)


def _dft_forward(src_ref, g_ref, m2r_ref, m2i_ref, xr, xi, out_r, out_i, scale, plan, jcols):
    n2, k1n = plan.n2, plan.k1

    def stage1(j2, c):
        v = src_ref[pl.ds(j2, jcols, stride=n2), :].astype(BF16)
        p = jnp.dot(g_ref[j2], v, preferred_element_type=F32)
        st = pl.multiple_of(j2 * k1n, 16)
        xr[pl.ds(st, k1n), :] = p[:k1n]
        xi[pl.ds(st, k1n), :] = p[k1n:]
        return c

    lax.fori_loop(0, n2, stage1, 0, unroll=DFT_UNROLL)

    def stage2(k, c):
        st = pl.multiple_of(k * n2, 16)
        yr = xr[pl.ds(k, n2, stride=k1n), :].astype(BF16)
        yi = xi[pl.ds(k, n2, stride=k1n), :].astype(BF16)
        z = (jnp.dot(m2r_ref[...], yr, preferred_element_type=F32)
             + jnp.dot(m2i_ref[...], yi, preferred_element_type=F32))
        if scale is not None:
            z = z * scale
        out_r[0, pl.ds(st, n2), :] = z[:n2].astype(BF16)
        out_i[0, pl.ds(st, n2), :] = z[n2:].astype(BF16)
        return c

    lax.fori_loop(0, k1n, stage2, 0, unroll=DFT_UNROLL)


def _hy_filter_fft_kernel(hi_ref, lo_ref, norm_ref, g_ref, m2r_ref, m2i_ref, hr_ref, hi_out_ref,
                          fbuf, xr, xi, *, plan):
    ns, n = plan.ns, plan.n
    rows = fbuf.shape[0]
    fbuf[pl.ds(0, ns), :] = hi_ref[...]
    fbuf[pl.ds(ns, rows - ns), :] = jnp.zeros((rows - ns, fbuf.shape[1]), F32)
    fbuf[pl.ds(n - ns, ns), :] = lo_ref[...]
    fbuf[pl.ds(n - ns, 1), :] = jnp.zeros((1, fbuf.shape[1]), F32)
    inv_norm = 1.0 / norm_ref[0:1, :]
    _dft_forward(fbuf, g_ref, m2r_ref, m2i_ref, xr, xi, hr_ref, hi_out_ref, inv_norm, plan, plan.j1f)


def _hy_filter_fft(hl, norm, plan, tabs):
    dh = hl.shape[1]
    cb_ = LANES
    nbd = 2 * plan.nb - 1
    ns = plan.ns
    spec_out = pl.BlockSpec((1, plan.rows, cb_), lambda p, c: (p, 0, c))
    return pl.pallas_call(
        functools.partial(_hy_filter_fft_kernel, plan=plan),
        grid=(nbd, dh // cb_),
        in_specs=[pl.BlockSpec((ns, cb_), lambda p, c: (p + 1, c)),
                  pl.BlockSpec((ns, cb_), lambda p, c: (p, c)),
                  pl.BlockSpec((8, cb_), lambda p, c: (0, c)),
                  _const_spec(tabs["g_filt"].shape), _const_spec(tabs["m2f_r"].shape),
                  _const_spec(tabs["m2f_i"].shape)],
        out_specs=[spec_out, spec_out],
        out_shape=[jax.ShapeDtypeStruct((nbd, plan.rows, dh), BF16)] * 2,
        scratch_shapes=[pltpu.VMEM((plan.j1f * plan.n2, cb_), F32),
                        pltpu.VMEM((plan.rows, cb_), F32), pltpu.VMEM((plan.rows, cb_), F32)],
        compiler_params=_params(("parallel", "parallel")),
        name="hyena_filter_fft",
    )(hl, hl, norm, tabs["g_filt"], tabs["m2f_r"], tabs["m2f_i"])


def _hy_fwd_fft_kernel(s_ref, g_ref, m2r_ref, m2i_ref, xr_out, xi_out, sbuf, xr, xi, *, plan):
    ns = plan.ns
    rows = sbuf.shape[0]
    sbuf[pl.ds(0, ns), :] = s_ref[0]
    if rows > ns:
        sbuf[pl.ds(ns, rows - ns), :] = jnp.zeros((rows - ns, sbuf.shape[1]), F32)
    _dft_forward(sbuf, g_ref, m2r_ref, m2i_ref, xr, xi, xr_out, xi_out, None, plan, plan.j1)


def _hy_fwd_fft(s_blocks, plan, tabs):
    q, ns, dh = s_blocks.shape
    cb_ = LANES
    spec_out = pl.BlockSpec((1, plan.rows, cb_), lambda b, c: (b, 0, c))
    return pl.pallas_call(
        functools.partial(_hy_fwd_fft_kernel, plan=plan),
        grid=(q, dh // cb_),
        in_specs=[pl.BlockSpec((1, ns, cb_), lambda b, c: (b, 0, c)),
                  _const_spec(tabs["g_sig"].shape), _const_spec(tabs["m2f_r"].shape),
                  _const_spec(tabs["m2f_i"].shape)],
        out_specs=[spec_out, spec_out],
        out_shape=[jax.ShapeDtypeStruct((q, plan.rows, dh), BF16)] * 2,
        scratch_shapes=[pltpu.VMEM((plan.j1 * plan.n2, cb_), F32),
                        pltpu.VMEM((plan.rows, cb_), F32), pltpu.VMEM((plan.rows, cb_), F32)],
        compiler_params=_params(("parallel", "parallel")),
        name="hyena_fwd_fft",
    )(s_blocks, tabs["g_sig"], tabs["m2f_r"], tabs["m2f_i"])


def _hy_mac_kernel(xr_ref, xi_ref, hr_ref, hi_ref, yr_ref, yi_ref, *, nb):
    for a in range(nb):
        yr = None
        for b in range(nb):
            d = a - b + nb - 1
            xr, xi = xr_ref[0, b].astype(F32), xi_ref[0, b].astype(F32)
            hr, hi = hr_ref[d].astype(F32), hi_ref[d].astype(F32)
            pr = xr * hr - xi * hi
            pi = xr * hi + xi * hr
            yr, yi = (pr, pi) if yr is None else (yr + pr, yi + pi)
        yr_ref[0, a] = yr.astype(BF16)
        yi_ref[0, a] = yi.astype(BF16)


def _hy_mac(xr, xi, hr, hi, batch, plan):
    nb = plan.nb
    dh = xr.shape[-1]
    rows = plan.rows
    rc = _pick_tile(rows, (256, 128, 64, 32, 16))
    cc = _pick_tile(dh, (1024, 512, 256, 128))
    x4 = lambda a: a.reshape(batch, nb, rows, dh)
    xs = pl.BlockSpec((1, nb, rc, cc), lambda b, r, c: (b, 0, r, c))
    hs = pl.BlockSpec((2 * nb - 1, rc, cc), lambda b, r, c: (0, r, c))
    yr, yi = pl.pallas_call(
        functools.partial(_hy_mac_kernel, nb=nb),
        grid=(batch, rows // rc, dh // cc),
        in_specs=[xs, xs, hs, hs],
        out_specs=[xs, xs],
        out_shape=[jax.ShapeDtypeStruct((batch, nb, rows, dh), BF16)] * 2,
        compiler_params=_params(("parallel", "parallel", "parallel")),
        name="hyena_freq_mac",
    )(x4(xr), x4(xi), hr, hi)
    return yr.reshape(batch * nb, rows, dh), yi.reshape(batch * nb, rows, dh)


def _hy_inv_fft_kernel(*refs, plan, with_filter):
    if with_filter:
        yr_ref, yi_ref, hr_ref, hi_ref, m2r_ref, m2i_ref, gr_ref, gi_ref, y_ref, ur, ui, ybuf = refs
    else:
        yr_ref, yi_ref, m2r_ref, m2i_ref, gr_ref, gi_ref, y_ref, ur, ui, ybuf = refs
    n2, k1n, j1 = plan.n2, plan.k1, plan.j1

    def stage2(k, c):
        st = pl.multiple_of(k * n2, 16)
        zr = yr_ref[0, pl.ds(st, n2), :]
        zi = yi_ref[0, pl.ds(st, n2), :]
        if with_filter:
            xr, xi = zr.astype(F32), zi.astype(F32)
            hr, hi = hr_ref[0, pl.ds(st, n2), :].astype(F32), hi_ref[0, pl.ds(st, n2), :].astype(F32)
            zr = (xr * hr - xi * hi).astype(BF16)
            zi = (xr * hi + xi * hr).astype(BF16)
        w = (jnp.dot(m2r_ref[...], zr, preferred_element_type=F32)
             + jnp.dot(m2i_ref[...], zi, preferred_element_type=F32))
        ur[pl.ds(st, n2), :] = w[:n2]
        ui[pl.ds(st, n2), :] = w[n2:]
        return c

    lax.fori_loop(0, k1n, stage2, 0, unroll=DFT_UNROLL)

    def stage1(j2, c):
        vr = ur[pl.ds(j2, k1n, stride=n2), :].astype(BF16)
        vi = ui[pl.ds(j2, k1n, stride=n2), :].astype(BF16)
        y = (jnp.dot(gr_ref[j2], vr, preferred_element_type=F32)
             + jnp.dot(gi_ref[j2], vi, preferred_element_type=F32))
        ybuf[pl.ds(j2, j1, stride=n2), :] = y
        return c

    lax.fori_loop(0, n2, stage1, 0, unroll=DFT_UNROLL)
    y_ref[0] = ybuf[pl.ds(0, plan.ns), :]


def _hy_inv_fft(yr, yi, plan, tabs, filt=None):
    q, rows, dh = yr.shape
    cb_ = LANES
    spec_in = pl.BlockSpec((1, rows, cb_), lambda b, c: (b, 0, c))
    filt_specs = [pl.BlockSpec((1, rows, cb_), lambda b, c: (0, 0, c))] * 2 if filt is not None else []
    return pl.pallas_call(
        functools.partial(_hy_inv_fft_kernel, plan=plan, with_filter=filt is not None),
        grid=(q, dh // cb_),
        in_specs=[spec_in, spec_in] + filt_specs
        + [_const_spec(tabs["m2i_r"].shape), _const_spec(tabs["m2i_i"].shape),
           _const_spec(tabs["gi_r"].shape), _const_spec(tabs["gi_i"].shape)],
        out_specs=pl.BlockSpec((1, plan.ns, cb_), lambda b, c: (b, 0, c)),
        out_shape=jax.ShapeDtypeStruct((q, plan.ns, dh), F32),
        scratch_shapes=[pltpu.VMEM((rows, cb_), F32), pltpu.VMEM((rows, cb_), F32),
                        pltpu.VMEM((plan.j1 * plan.n2, cb_), F32)],
        compiler_params=_params(("parallel", "parallel")),
        name="hyena_inv_fft",
    )(yr, yi, *(filt or ()), tabs["m2i_r"], tabs["m2i_i"], tabs["gi_r"], tabs["gi_i"])


def _rope_tables(seq_len):
    half = QK_ROPE // 2
    inv = 1.0 / (ROPE_THETA ** (jnp.arange(0, QK_ROPE, 2, dtype=F32) / QK_ROPE))
    ang = jnp.arange(seq_len, dtype=F32)[:, None] * inv[None, :]
    cos2 = jnp.concatenate([jnp.cos(ang)] * 2, axis=-1)
    sin2 = jnp.concatenate([jnp.sin(ang)] * 2, axis=-1)
    scale = (QK_NOPE + QK_ROPE) ** -0.5 * math.log2(math.e)
    zpad = jnp.zeros((seq_len, HEAD_PAD - QK_NOPE - QK_ROPE), F32)
    ones = jnp.ones((seq_len, QK_NOPE), F32)
    zer = jnp.zeros((seq_len, QK_NOPE), F32)
    zk = jnp.zeros((seq_len, LANES - QK_ROPE), F32)
    del half
    return jnp.concatenate([scale * ones, scale * cos2, zpad, zer, scale * sin2, zpad,
                            cos2, zk, sin2, zk], axis=-1)


def _mla_prep_kernel(pc_ref, tab_ref, gq_ref, gkv_ref, wa_ref, wb_ref, wkn_ref, wv_ref,
                     q_ref, k_ref, v_ref, kmax_ref, *, ql, kvl, seq_len, tb):
    hp = HEAD_PAD
    i = pl.program_id(1)
    live = (i * tb + lax.broadcasted_iota(I32, (tb, 1), 0)) < seq_len
    lane8 = lax.broadcasted_iota(I32, (8, LANES), 1)

    @pl.when(i == 0)
    def _():
        kmax_ref[...] = jnp.zeros_like(kmax_ref)

    pc = pc_ref[0].astype(F32)
    tab = tab_ref[...]
    qc, qs = tab[:, 0:hp], tab[:, hp:2 * hp]
    kc, ks = tab[:, 2 * hp:2 * hp + LANES], tab[:, 2 * hp + LANES:2 * hp + 2 * LANES]

    def rms(x, g):
        return (x * lax.rsqrt(jnp.mean(x * x, axis=-1, keepdims=True) + RMS_EPS) * g).astype(BF16)

    cq = rms(pc[:, 0:ql], gq_ref[...])
    ckv = rms(pc[:, ql:ql + kvl], gkv_ref[...])
    kr = pc[:, ql + kvl:ql + kvl + LANES] * kc + pc[:, ql + kvl + LANES:ql + kvl + 2 * LANES] * ks
    kr = kr.astype(BF16)
    kr_sq = jnp.sum(jnp.square(kr.astype(F32)), axis=-1, keepdims=True)
    lane_t = lax.broadcasted_iota(I32, (tb, hp - V_DIM), 1)
    ones_col = jnp.where(lane_t == 0, 1.0, 0.0).astype(BF16)
    k_tail = jnp.where(lane_t == QK_ROPE, 1.0, kr.astype(F32)).astype(BF16)
    kmax = kmax_ref[0]
    for h in range(N_HEADS):
        qa = jnp.dot(cq, wa_ref[:, h * hp:(h + 1) * hp], preferred_element_type=F32)
        qb = jnp.dot(cq, wb_ref[:, h * hp:(h + 1) * hp], preferred_element_type=F32)
        q_ref[0, :, h * hp:(h + 1) * hp] = (qa * qc + qb * qs).astype(BF16)
        kn = jnp.dot(ckv, wkn_ref[:, h * QK_NOPE:(h + 1) * QK_NOPE], preferred_element_type=F32).astype(BF16)
        k_ref[0, :, h * hp:h * hp + QK_NOPE] = kn
        k_ref[0, :, h * hp + QK_NOPE:(h + 1) * hp] = k_tail
        k_sq = jnp.sum(jnp.square(kn.astype(F32)), axis=-1, keepdims=True) + kr_sq
        k_sq_max = jnp.max(jnp.where(live, k_sq, 0.0), axis=0, keepdims=True)
        kmax = jnp.where(lane8 == h, jnp.maximum(kmax, k_sq_max), kmax)
        vv = jnp.dot(ckv, wv_ref[:, h * V_DIM:(h + 1) * V_DIM], preferred_element_type=F32)
        v_ref[0, :, h * hp:h * hp + V_DIM] = vv.astype(BF16)
        v_ref[0, :, h * hp + V_DIM:(h + 1) * hp] = ones_col
    kmax_ref[0] = kmax


def _mla_prep(pc3, tab, gq, gkv, wa, wb, wkn, wv):
    B, L, pcw = pc3.shape
    ql, kvl = gq.shape[-1], gkv.shape[-1]
    tb = ROW_BLOCK
    hq = N_HEADS * HEAD_PAD
    hv = N_HEADS * HEAD_PAD
    blk = lambda w: pl.BlockSpec((1, tb, w), lambda b, i: (b, i, 0))
    return pl.pallas_call(
        functools.partial(_mla_prep_kernel, ql=ql, kvl=kvl, seq_len=L, tb=tb),
        grid=(B, pl.cdiv(L, tb)),
        in_specs=[blk(pcw), pl.BlockSpec((tb, tab.shape[1]), lambda b, i: (i, 0)),
                  _const_spec((1, ql)), _const_spec((1, kvl)),
                  _const_spec(wa.shape), _const_spec(wb.shape), _const_spec(wkn.shape), _const_spec(wv.shape)],
        out_specs=[blk(hq), blk(hq), blk(hv), pl.BlockSpec((1, 8, LANES), lambda b, i: (b, 0, 0))],
        out_shape=[jax.ShapeDtypeStruct((B, L, hq), BF16), jax.ShapeDtypeStruct((B, L, hq), BF16),
                   jax.ShapeDtypeStruct((B, L, hv), BF16), jax.ShapeDtypeStruct((B, 8, LANES), F32)],
        compiler_params=_params(("parallel", "arbitrary")),
        name="mla_prep",
    )(pc3, tab, gq.reshape(1, ql), gkv.reshape(1, kvl), wa, wb, wkn, wv)


def _attn_kernel(q_ref, k_ref, v_ref, kmax_ref, o_ref, *, seq_len, ck, unroll):
    h = pl.program_id(1)
    i = pl.program_id(2)
    q = q_ref[0]
    tq = q.shape[0]
    nt = (((1,), (1,)), ((), ()))
    full = seq_len // ck
    lead = seq_len % ck

    def over_chunks(step, state, per_trip):
        if lead:
            state = step(0, lead, state)
        for c in range(full % per_trip):
            state = step(lead + c * ck, ck, state)
        base = lead + (full % per_trip) * ck

        def body(it, st):
            for c in range(per_trip):
                st = step(pl.multiple_of(base + (it * per_trip + c) * ck, 16), ck, st)
            return st

        return lax.fori_loop(0, full // per_trip, body, state)

    qf = q.astype(F32)
    q_norm = jnp.sqrt(jnp.sum(qf * qf, axis=-1, keepdims=True))
    lane8 = lax.broadcasted_iota(I32, (8, LANES), 1)
    k_sq_max = jnp.max(jnp.where(lane8 == h, kmax_ref[0], 0.0), axis=(0, 1), keepdims=True)
    bound = q_norm * jnp.sqrt(k_sq_max)
    lane = lax.broadcasted_iota(I32, (tq, HEAD_PAD), 1)
    q_shift = jnp.where(lane == QK_NOPE + QK_ROPE, (-bound).astype(BF16), q)

    def fast(start, size, acc):
        s = lax.dot_general(q_shift, k_ref[0, pl.ds(start, size), :], nt, preferred_element_type=F32)
        p = jnp.exp2(s).astype(BF16)
        return acc + jnp.dot(p, v_ref[0, pl.ds(start, size), :], preferred_element_type=F32)

    acc = over_chunks(fast, jnp.zeros((tq, HEAD_PAD), F32), unroll)
    denom = acc[:, V_DIM:V_DIM + 1]
    o_ref[0] = (acc[:, :V_DIM] / denom).astype(BF16)

    rows = i * tq + lax.broadcasted_iota(I32, (tq, 1), 0)
    healthy = jnp.where(rows < seq_len, denom, 1.0) >= ATT_MIN_DENOM
    n_bad = jnp.sum(jnp.where(healthy, 0.0, 1.0))

    @pl.when(n_bad > 0.0)
    def _():
        def exact(start, size, state):
            m, acc_e = state
            s = lax.dot_general(q, k_ref[0, pl.ds(start, size), :], nt, preferred_element_type=F32)
            m_new = jnp.maximum(m, jnp.max(s, axis=-1, keepdims=True))
            p = jnp.exp2(s - m_new).astype(BF16)
            acc_e = jnp.exp2(m - m_new) * acc_e + jnp.dot(
                p, v_ref[0, pl.ds(start, size), :], preferred_element_type=F32)
            return m_new, acc_e

        _, acc_e = over_chunks(exact, (jnp.full((tq, 1), -1e30, F32), jnp.zeros((tq, HEAD_PAD), F32)),
                               ATT_EXACT_UNROLL)
        o_ref[0] = (acc_e[:, :V_DIM] / acc_e[:, V_DIM:V_DIM + 1]).astype(BF16)


def _attention(q, k, v, kmax):
    B, L, _ = q.shape
    tq = ATT_TQ
    ck = min(ATT_CK, L)
    return pl.pallas_call(
        functools.partial(_attn_kernel, seq_len=L, ck=ck, unroll=ATT_UNROLL),
        grid=(B, N_HEADS, pl.cdiv(L, tq)),
        in_specs=[pl.BlockSpec((1, tq, HEAD_PAD), lambda b, h, i: (b, i, h)),
                  pl.BlockSpec((1, L, HEAD_PAD), lambda b, h, i: (b, 0, h)),
                  pl.BlockSpec((1, L, HEAD_PAD), lambda b, h, i: (b, 0, h)),
                  pl.BlockSpec((1, 8, LANES), lambda b, h, i: (b, 0, 0))],
        out_specs=pl.BlockSpec((1, tq, V_DIM), lambda b, h, i: (b, i, h)),
        out_shape=jax.ShapeDtypeStruct((B, L, N_HEADS * V_DIM), BF16),
        compiler_params=_params(("parallel", "parallel", "arbitrary")),
        name="mla_attention",
    )(q, k, v, kmax)


def _merge_kernel(h2_ref, gb_ref, y_ref, s_ref, x0_ref, skip_ref, o_ref, pg_ref,
                  wpa_ref, wpb_ref, wpc_ref, m_ref, *, d):
    gb = gb_ref[...].astype(F32)
    xa = (jax.nn.gelu(gb) * (h2_ref[0] + h2_ref[1])).astype(BF16)
    xb = (x0_ref[...].astype(F32) * (y_ref[...] + s_ref[...] * skip_ref[...])).astype(BF16)
    ya = jnp.dot(xa, wpa_ref[...], preferred_element_type=F32)
    yb = jnp.dot(xb, wpb_ref[...], preferred_element_type=F32)
    yc = jnp.dot(o_ref[...], wpc_ref[...], preferred_element_type=F32)
    g = jax.nn.sigmoid(pg_ref[...].astype(F32))
    m_ref[...] = (g[:, 0:d] * ya + g[:, d:2 * d] * yb + g[:, 2 * d:3 * d] * yc).astype(BF16)


def _merge(h2, pa, y, s, x0, skip, o, pg, wpa, wpb, wpc):
    T = o.shape[0]
    d = wpa.shape[1]
    dr, dh = wpa.shape[0], wpb.shape[0]
    rb = MERGE_ROW_BLOCK
    row = lambda w, col=0: pl.BlockSpec((rb, w), lambda i: (i, col))
    return pl.pallas_call(
        functools.partial(_merge_kernel, d=d),
        grid=(pl.cdiv(T, rb),),
        in_specs=[pl.BlockSpec((2, rb, dr), lambda i: (0, i, 0)), row(dr, 1), row(dh), row(dh), row(dh),
                  _const_spec((1, dh)), row(o.shape[1]), row(3 * d),
                  _const_spec(wpa.shape), _const_spec(wpb.shape), _const_spec(wpc.shape)],
        out_specs=row(d),
        out_shape=jax.ShapeDtypeStruct((T, d), BF16),
        compiler_params=_params(("parallel",)),
        name="branch_merge",
    )(h2, pa, y, s, x0, skip.reshape(1, dh), o, pg, wpa, wpb, wpc)


def _route(logits):
    ng, ne = N_GROUPS, N_GROUPS * EXPERTS_PER_GROUP
    lane = lax.broadcasted_iota(I32, logits.shape, 1)
    neg = jnp.float32(-jnp.inf)
    big = jnp.int32(4 * LANES)
    gl = jnp.where(lane < ng, logits, neg)
    gmax = jnp.max(gl, axis=-1, keepdims=True)
    g_sel = jnp.min(jnp.where(gl == gmax, lane, big), axis=-1, keepdims=True)
    p_sel = 1.0 / jnp.sum(jnp.exp(gl - gmax), axis=-1, keepdims=True)
    e_lane = lane - ng
    grp_shift = EXPERTS_PER_GROUP.bit_length() - 1
    in_grp = (e_lane >= 0) & (e_lane < ne) & (lax.shift_right_arithmetic(e_lane, grp_shift) == g_sel)
    el = jnp.where(in_grp, logits, neg)
    v0 = jnp.max(el, axis=-1, keepdims=True)
    i0 = jnp.min(jnp.where(el == v0, lane, big), axis=-1, keepdims=True)
    el1 = jnp.where(lane == i0, neg, el)
    v1 = jnp.max(el1, axis=-1, keepdims=True)
    i1 = jnp.min(jnp.where(el1 == v1, lane, big), axis=-1, keepdims=True)
    t = jnp.exp(v1 - v0)
    w0 = p_sel / (1.0 + t)
    w1 = p_sel * t / (1.0 + t)
    e0 = (i0 - ng).astype(F32)
    e1 = (i1 - ng).astype(F32)
    return jnp.where(lane == 0, e0, jnp.where(lane == 1, e1, jnp.where(lane == 2, w0, jnp.where(lane == 3, w1, 0.0))))


def _outproj_kernel(m_ref, h_ref, wout_ref, g_ref, b_ref, wr_hi_ref, wr_lo_ref, br_ref,
                    h1_ref, route_ref, *, alpha):
    mixed = jnp.dot(m_ref[...], wout_ref[...], preferred_element_type=F32)
    h1 = _layer_norm_val(alpha * h_ref[...] + mixed, g_ref[...], b_ref[...])
    h1_ref[...] = h1
    hi = h1.astype(BF16)
    lo = (h1 - hi.astype(F32)).astype(BF16)
    logits = (jnp.dot(hi, wr_hi_ref[...], preferred_element_type=F32)
              + jnp.dot(lo, wr_hi_ref[...], preferred_element_type=F32)
              + jnp.dot(hi, wr_lo_ref[...], preferred_element_type=F32)) + br_ref[...]
    route_ref[...] = _route(logits)


def _outproj(m, h, wout, g, b, wr_hi, wr_lo, br, alpha):
    T, d = h.shape
    rb = ROW_BLOCK
    row = lambda w: pl.BlockSpec((rb, w), lambda i: (i, 0))
    return pl.pallas_call(
        functools.partial(_outproj_kernel, alpha=alpha),
        grid=(pl.cdiv(T, rb),),
        in_specs=[row(d), row(d), _const_spec(wout.shape), _const_spec((1, d)), _const_spec((1, d)),
                  _const_spec(wr_hi.shape), _const_spec(wr_lo.shape), _const_spec((1, LANES))],
        out_specs=[row(d), row(LANES)],
        out_shape=[jax.ShapeDtypeStruct((T, d), F32), jax.ShapeDtypeStruct((T, LANES), F32)],
        compiler_params=_params(("parallel",)),
        name="outproj_ln_router",
    )(m, h, wout, g.reshape(1, d), b.reshape(1, d), wr_hi, wr_lo, br)


def _lane_cumsum(x):
    lane = lax.broadcasted_iota(I32, x.shape, 1)
    s = 1
    while s < x.shape[1]:
        x = x + jnp.where(lane >= s, pltpu.roll(x, s, axis=1), 0.0)
        s *= 2
    return x


def _assign_onehots(route, i, tb, tokens):
    lane = lax.broadcasted_iota(I32, (tb, LANES), 1)
    row = i * tb + lax.broadcasted_iota(I32, (tb, 1), 0)
    live = row < tokens
    e0 = jnp.where(live, route[:, 0:1], -1.0).astype(I32)
    e1 = jnp.where(live, route[:, 1:2], -1.0).astype(I32)
    oh0 = jnp.where(lane == e0, 1.0, 0.0)
    oh1 = jnp.where(lane == e1, 1.0, 0.0)
    return oh0, oh1


def _count_kernel(route_ref, counts_ref, *, tokens, tb):
    i = pl.program_id(0)
    oh0, oh1 = _assign_onehots(route_ref[...], i, tb, tokens)

    @pl.when(i == 0)
    def _():
        counts_ref[...] = jnp.zeros_like(counts_ref)

    counts_ref[0:1, :] += jnp.sum(oh0 + oh1, axis=0, keepdims=True)


def _rank_kernel(route_ref, counts_ref, dest_ref, blk_ref, base, *, tokens, tb, blk, nblocks):
    i = pl.program_id(0)
    ne = N_GROUPS * EXPERTS_PER_GROUP
    lane = lax.broadcasted_iota(I32, (tb, LANES), 1)
    oh0, oh1 = _assign_onehots(route_ref[...], i, tb, tokens)
    both = oh0 + oh1

    @pl.when(i == 0)
    def _():
        lane1 = lax.broadcasted_iota(I32, (1, LANES), 1)
        cnt = jnp.where(lane1 < ne, counts_ref[0:1, :], 0.0)
        padded = jnp.ceil(cnt / blk) * blk
        pend = _lane_cumsum(padded)
        base[...] = pend - padded
        jrow = lax.broadcasted_iota(I32, (nblocks, LANES), 0).astype(F32) * blk
        lanej = lax.broadcasted_iota(I32, (nblocks, LANES), 1)
        ended = jnp.where((lanej < ne) & (pend <= jrow), 1.0, 0.0)
        bexp = jnp.minimum(jnp.sum(ended, axis=-1, keepdims=True), ne - 1.0)
        used = jnp.sum(jnp.where(lane1 == ne - 1, pend, 0.0), axis=-1, keepdims=True) / blk
        blk_ref[...] = jnp.where(lanej == 0, bexp, jnp.where(lanej == 1, used, 0.0))

    r_i = lax.broadcasted_iota(I32, (tb, tb), 0)
    c_i = lax.broadcasted_iota(I32, (tb, tb), 1)
    ltri = jnp.where(c_i < r_i, 1.0, 0.0).astype(BF16)
    prefix = jnp.dot(ltri, both.astype(BF16), preferred_element_type=F32)
    slot = base[...] + prefix
    d0 = jnp.sum(oh0 * slot, axis=-1, keepdims=True)
    d1 = jnp.sum(oh1 * slot, axis=-1, keepdims=True)
    dest_ref[...] = jnp.where(lane == 0, d0, jnp.where(lane == 1, d1, 0.0))
    base[...] += jnp.sum(both, axis=0, keepdims=True)


def _dispatch(route, blk):
    T = route.shape[0]
    tb = ROW_BLOCK
    ne = N_GROUPS * EXPERTS_PER_GROUP
    nblocks = (2 * T + ne * (blk - 1) + blk - 1) // blk
    nbp = _round_up(nblocks, 8)
    rspec = pl.BlockSpec((tb, LANES), lambda i: (i, 0))
    cspec = pl.BlockSpec((8, LANES), lambda i: (0, 0))
    counts = pl.pallas_call(
        functools.partial(_count_kernel, tokens=T, tb=tb),
        grid=(pl.cdiv(T, tb),),
        in_specs=[rspec],
        out_specs=cspec,
        out_shape=jax.ShapeDtypeStruct((8, LANES), F32),
        compiler_params=_params(("arbitrary",)),
        name="moe_counts",
    )(route)
    dest, blkinfo = pl.pallas_call(
        functools.partial(_rank_kernel, tokens=T, tb=tb, blk=blk, nblocks=nbp),
        grid=(pl.cdiv(T, tb),),
        in_specs=[rspec, cspec],
        out_specs=[rspec, pl.BlockSpec((nbp, LANES), lambda i: (0, 0))],
        out_shape=[jax.ShapeDtypeStruct((T, LANES), F32), jax.ShapeDtypeStruct((nbp, LANES), F32)],
        scratch_shapes=[pltpu.VMEM((1, LANES), F32)],
        compiler_params=_params(("arbitrary",)),
        name="moe_rank",
    )(route, counts)
    return dest, blkinfo, nblocks


def _for_row_groups(n_rows, fn):
    def body(g, c):
        for j in range(DMA_GROUP):
            fn(g * DMA_GROUP + j)
        return c

    lax.fori_loop(0, n_rows // DMA_GROUP, body, 0)


def _scatter_kernel(dest_ref, x_ref, xs_in, xs_out, sem, *, tb, tokens):
    del xs_in
    live_rows = jnp.minimum(tb, tokens - pl.program_id(0) * tb)

    def copy(r, s):
        d = dest_ref[0, 0, 2 * r + s]
        return pltpu.make_async_copy(x_ref.at[pl.ds(r, 1)], xs_out.at[pl.ds(d, 1)], sem)

    def start(r):
        copy(r, 0).start()
        copy(r, 1).start()

    def wait(r):
        copy(r, 0).wait()
        copy(r, 1).wait()

    _for_row_groups(live_rows, start)
    _for_row_groups(live_rows, wait)


def _moe_scatter(h1, dest_idx, rows, tb):
    T, d = h1.shape
    assert T % DMA_GROUP == 0
    nblk = dest_idx.shape[0]
    xs0 = jnp.zeros((rows, d), F32)
    return pl.pallas_call(
        functools.partial(_scatter_kernel, tb=tb, tokens=T),
        grid=(nblk,),
        in_specs=[pl.BlockSpec((1, 1, 2 * tb), lambda i: (i, 0, 0), memory_space=pltpu.SMEM),
                  pl.BlockSpec((tb, d), lambda i: (i, 0)),
                  pl.BlockSpec(memory_space=pl.ANY)],
        out_specs=pl.BlockSpec(memory_space=pl.ANY),
        out_shape=jax.ShapeDtypeStruct((rows, d), F32),
        scratch_shapes=[pltpu.SemaphoreType.DMA(())],
        input_output_aliases={2: 0},
        compiler_params=_params(("arbitrary",)),
        name="moe_scatter",
    )(dest_idx, h1, xs0)


def _expert_kernel(be_ref, nu_ref, ly_ref, x_ref, wg_ref, wu_ref, wd_ref, y_ref):
    del be_ref, ly_ref
    i = pl.program_id(0)

    @pl.when(i < nu_ref[0])
    def _():
        x = x_ref[...].astype(BF16)
        g = jnp.dot(x, wg_ref[0, 0], preferred_element_type=F32)
        u = jnp.dot(x, wu_ref[0, 0], preferred_element_type=F32)
        mid = (g * jax.nn.sigmoid(g) * u).astype(BF16)
        y_ref[...] = jnp.dot(mid, wd_ref[0, 0], preferred_element_type=F32)

    @pl.when(i >= nu_ref[0])
    def _():
        y_ref[...] = jnp.zeros_like(y_ref)


def _moe_experts(xs, blk_e, n_used, layer, wg, wu, wd, blk, nblocks):
    d = xs.shape[1]
    de = wg.shape[-1]
    grid_spec = pltpu.PrefetchScalarGridSpec(
        num_scalar_prefetch=3,
        grid=(nblocks,),
        in_specs=[pl.BlockSpec((blk, d), lambda i, be, nu, ly: (i, 0)),
                  pl.BlockSpec((1, 1, d, de), lambda i, be, nu, ly: (ly[0], be[i], 0, 0)),
                  pl.BlockSpec((1, 1, d, de), lambda i, be, nu, ly: (ly[0], be[i], 0, 0)),
                  pl.BlockSpec((1, 1, de, d), lambda i, be, nu, ly: (ly[0], be[i], 0, 0))],
        out_specs=pl.BlockSpec((blk, d), lambda i, be, nu, ly: (i, 0)),
    )
    return pl.pallas_call(
        _expert_kernel,
        grid_spec=grid_spec,
        out_shape=jax.ShapeDtypeStruct((nblocks * blk, d), F32),
        compiler_params=_params(("arbitrary",)),
        name="moe_experts",
    )(blk_e, n_used, layer, xs, wg, wu, wd)


def _combine_kernel(dest_ref, h_ref, route_ref, g_ref, b_ref, ys_hbm, o_ref, ob_ref, gbuf, sem, *, tb, alpha):
    def copy(r, s):
        d = dest_ref[0, 0, 2 * r + s]
        return pltpu.make_async_copy(ys_hbm.at[pl.ds(d, 1)], gbuf.at[s, pl.ds(r, 1)], sem)

    def start(r):
        copy(r, 0).start()
        copy(r, 1).start()

    def wait(r):
        copy(r, 0).wait()
        copy(r, 1).wait()

    _for_row_groups(tb, start)
    _for_row_groups(tb, wait)
    route = route_ref[...]
    moe = route[:, 2:3] * gbuf[0] + route[:, 3:4] * gbuf[1]
    y = _layer_norm_val(alpha * h_ref[...] + moe, g_ref[...], b_ref[...])
    o_ref[...] = y
    ob_ref[...] = y.astype(BF16)


def _moe_combine(h1, route, dest_idx, ys, g, b, alpha, tb):
    T, d = h1.shape
    nblk = dest_idx.shape[0]
    row = lambda w: pl.BlockSpec((tb, w), lambda i: (i, 0))
    return pl.pallas_call(
        functools.partial(_combine_kernel, tb=tb, alpha=alpha),
        grid=(nblk,),
        in_specs=[pl.BlockSpec((1, 1, 2 * tb), lambda i: (i, 0, 0), memory_space=pltpu.SMEM),
                  row(d), row(LANES), _const_spec((1, d)), _const_spec((1, d)),
                  pl.BlockSpec(memory_space=pl.ANY)],
        out_specs=[row(d), row(d)],
        out_shape=[jax.ShapeDtypeStruct((T, d), F32), jax.ShapeDtypeStruct((T, d), BF16)],
        scratch_shapes=[pltpu.VMEM((2, tb, d), F32), pltpu.SemaphoreType.DMA(())],
        compiler_params=_params(("arbitrary",)),
        name="moe_combine_ln",
    )(dest_idx, h1, route, g.reshape(1, d), b.reshape(1, d), ys)


def _moe(h1, route, layer, wg, wu, wd, g, b, alpha):
    T, d = h1.shape
    blk = MOE_BLOCK
    tb = MOE_TOKEN_BLOCK
    dest, blkinfo, nblocks = _dispatch(route, blk)
    nblk_tok = pl.cdiv(T, tb)
    di = dest[:, 0:2].astype(I32)
    dest_idx = jnp.pad(di, ((0, nblk_tok * tb - T), (0, 0))).reshape(nblk_tok, 1, 2 * tb)
    blk_e = blkinfo[:nblocks, 0].astype(I32)
    n_used = blkinfo[0:1, 1].astype(I32)
    xs = _moe_scatter(h1, dest_idx, nblocks * blk, tb)
    ys = _moe_experts(xs, blk_e, n_used, layer, wg, wu, wd, blk, nblocks)
    return _moe_combine(h1, route, dest_idx, ys, g, b, alpha, tb)


def _prep_weights(W):
    depth, d, _ = W["w_in"].shape
    dr = W["conv_a_w"].shape[-1]
    dh = W["conv_b_w"].shape[-1] // 3
    ql, kvl = W["q_norm_g"].shape[-1], W["kv_norm_g"].shape[-1]
    ca, cbw = 2 * dr, 3 * dh
    cc = ql + kvl + QK_ROPE
    w_in = W["w_in"]
    half = QK_ROPE // 2

    def rot(w):
        return jnp.concatenate([-w[..., half:], w[..., :half]], axis=-1)

    w_a = w_in[..., :ca]
    w_b = w_in[..., ca:ca + cbw]
    w_c = w_in[..., ca + cbw:ca + cbw + cc]
    w_g = w_in[..., ca + cbw + cc:]
    w_kr = w_c[..., ql + kvl:]
    zpad = jnp.zeros(w_kr.shape[:-1] + (LANES - QK_ROPE,), F32)
    w_c_ext = jnp.concatenate([w_c[..., :ql + kvl], w_kr, zpad, rot(w_kr), zpad], axis=-1)

    wq = W["w_uq"].reshape(depth, ql, N_HEADS, QK_NOPE + QK_ROPE)
    z_n = jnp.zeros((depth, ql, N_HEADS, QK_NOPE), F32)
    z_p = jnp.zeros((depth, ql, N_HEADS, HEAD_PAD - QK_NOPE - QK_ROPE), F32)
    wa = jnp.concatenate([wq, z_p], axis=-1).reshape(depth, ql, N_HEADS * HEAD_PAD)
    wb = jnp.concatenate([z_n, rot(wq[..., QK_NOPE:]), z_p], axis=-1).reshape(depth, ql, N_HEADS * HEAD_PAD)
    wkv = W["w_ukv"].reshape(depth, kvl, N_HEADS, QK_NOPE + V_DIM)
    wkn = wkv[..., :QK_NOPE].reshape(depth, kvl, N_HEADS * QK_NOPE)
    wv = wkv[..., QK_NOPE:].reshape(depth, kvl, N_HEADS * V_DIM)

    ng, ne = N_GROUPS, N_GROUPS * EXPERTS_PER_GROUP
    wr = jnp.concatenate([W["w_rg"], W["w_re"], jnp.zeros((depth, d, LANES - ng - ne), F32)], axis=-1)
    br = jnp.concatenate([W["b_rg"], W["b_re"], jnp.zeros((depth, LANES - ng - ne), F32)], axis=-1)
    wr_hi = wr.astype(BF16)
    wr_lo = (wr - wr_hi.astype(F32)).astype(BF16)
    bf = lambda x: x.astype(BF16)
    return dict(
        w_a=bf(w_a), w_b=bf(w_b), w_c=bf(w_c_ext), w_g=bf(w_g),
        conv_a_w=W["conv_a_w"], conv_a_b=W["conv_a_b"], rg_wa=bf(W["rg_wa"]), rg_ba=W["rg_ba"],
        rg_wx=bf(W["rg_wx"]), rg_bx=W["rg_bx"], rg_lambda=W["rg_lambda"],
        conv_b_w=W["conv_b_w"], conv_b_b=W["conv_b_b"], hy_w1=W["hy_w1"], hy_b1=W["hy_b1"],
        hy_w2=W["hy_w2"], hy_b2=W["hy_b2"], hy_w3=W["hy_w3"], hy_freq=W["hy_freq"], hy_skip=W["hy_skip"],
        q_norm_g=W["q_norm_g"], kv_norm_g=W["kv_norm_g"], wa=bf(wa), wb=bf(wb), wkn=bf(wkn), wv=bf(wv),
        w_pa=bf(W["w_pa"]), w_pb=bf(W["w_pb"]), w_pc=bf(W["w_pc"]), w_out=bf(W["w_out"]),
        ln1_g=W["ln1_g"], ln1_b=W["ln1_b"], wr_hi=wr_hi, wr_lo=wr_lo, br=br.reshape(depth, 1, LANES),
        w_e_gate=bf(W["w_e_gate"]), w_e_up=bf(W["w_e_up"]), w_e_down=bf(W["w_e_down"]),
        ln2_g=W["ln2_g"], ln2_b=W["ln2_b"],
    )


def _encoder_layer(h, hb, P, experts, B, L, plan, tabs, rope_tab, alpha):
    T = B * L
    pa = _matmul(hb, P["w_a"], BF16, "in_proj_a")
    pb = _matmul(hb, P["w_b"], BF16, "in_proj_b")
    pc = _matmul(hb, P["w_c"], BF16, "in_proj_c")
    pg = _matmul(hb, P["w_g"], BF16, "in_proj_g")

    a, u = _rglru_gates(pa.reshape(B, L, -1), P["conv_a_w"], P["conv_a_b"], P["rg_wa"], P["rg_ba"],
                        P["rg_wx"], P["rg_bx"], P["rg_lambda"])
    h2 = _rglru_scan(a, u)

    s, x0 = _hy_conv3(pb.reshape(B, L, -1), P["conv_b_w"], P["conv_b_b"], plan)
    dh = s.shape[-1]
    hl, norm = _hy_taps(L, plan, P["hy_w1"], P["hy_b1"], P["hy_w2"], P["hy_b2"], P["hy_w3"], P["hy_freq"])
    f_r, f_i = _hy_filter_fft(hl, norm, plan, tabs)
    x_r, x_i = _hy_fwd_fft(s.reshape(B * plan.nb, plan.ns, dh), plan, tabs)
    if plan.nb == 1:
        y = _hy_inv_fft(x_r, x_i, plan, tabs, filt=(f_r, f_i))
    else:
        y_r, y_i = _hy_mac(x_r, x_i, f_r, f_i, B, plan)
        y = _hy_inv_fft(y_r, y_i, plan, tabs)
    y = y.reshape(B, plan.nb * plan.ns, dh)
    if plan.nb * plan.ns != L and B > 1:
        y, s, x0 = y[:, :L], s[:, :L], x0[:, :L]
    flat = lambda t: t.reshape(-1, dh)

    q, k, v, kmax = _mla_prep(pc.reshape(B, L, -1), rope_tab, P["q_norm_g"], P["kv_norm_g"],
                        P["wa"], P["wb"], P["wkn"], P["wv"])
    o = _attention(q, k, v, kmax).reshape(T, -1)

    m = _merge(h2, pa, flat(y), flat(s), flat(x0), P["hy_skip"], o, pg, P["w_pa"], P["w_pb"], P["w_pc"])
    h1, route = _outproj(m, h, P["w_out"], P["ln1_g"], P["ln1_b"], P["wr_hi"], P["wr_lo"], P["br"], alpha)
    return _moe(h1, route, P["layer"], experts["w_e_gate"], experts["w_e_up"], experts["w_e_down"],
                P["ln2_g"], P["ln2_b"], alpha)


def _encoder_trunk(x, meta, ln_g, ln_b, PW, alpha):
    B, S, D = x.shape
    L = S + N_META
    xc = jnp.concatenate([jnp.broadcast_to(meta[None], (B, N_META, D)), x], axis=1).reshape(B * L, D)
    h, hb = _layer_norm_rows(xc, ln_g, ln_b)
    plan = _FftPlan(L)
    tabs = plan.tables()
    rope_tab = _rope_tables(L)
    expert_names = ("w_e_gate", "w_e_up", "w_e_down")
    experts = {k: PW[k] for k in expert_names}
    per_layer = {k: v for k, v in PW.items() if k not in expert_names}
    per_layer["layer"] = jnp.arange(PW["w_a"].shape[0], dtype=I32).reshape(-1, 1)

    def body(carry, P):
        return _encoder_layer(carry[0], carry[1], P, experts, B, L, plan, tabs, rope_tab, alpha), None

    (h, hb), _ = lax.scan(body, (h, hb), per_layer)
    return h.reshape(B, L, D)[:, N_META:]


def kernel(x_prompt, x_sample, meta, ln_emb_g, ln_emb_b, w_in, conv_a_w, conv_a_b, rg_wa, rg_ba, rg_wx, rg_bx, rg_lambda, conv_b_w, conv_b_b, hy_w1, hy_b1, hy_w2, hy_b2, hy_w3, hy_freq, hy_skip, q_norm_g, w_uq, kv_norm_g, w_ukv, w_pa, w_pb, w_pc, w_out, ln1_g, ln1_b, w_rg, b_rg, w_re, b_re, w_e_gate, w_e_up, w_e_down, ln2_g, ln2_b):
    W = dict(w_in=w_in, conv_a_w=conv_a_w, conv_a_b=conv_a_b, rg_wa=rg_wa, rg_ba=rg_ba, rg_wx=rg_wx,
             rg_bx=rg_bx, rg_lambda=rg_lambda, conv_b_w=conv_b_w, conv_b_b=conv_b_b, hy_w1=hy_w1,
             hy_b1=hy_b1, hy_w2=hy_w2, hy_b2=hy_b2, hy_w3=hy_w3, hy_freq=hy_freq, hy_skip=hy_skip,
             q_norm_g=q_norm_g, w_uq=w_uq, kv_norm_g=kv_norm_g, w_ukv=w_ukv, w_pa=w_pa, w_pb=w_pb,
             w_pc=w_pc, w_out=w_out, ln1_g=ln1_g, ln1_b=ln1_b, w_rg=w_rg, b_rg=b_rg, w_re=w_re, b_re=b_re,
             w_e_gate=w_e_gate, w_e_up=w_e_up, w_e_down=w_e_down, ln2_g=ln2_g, ln2_b=ln2_b)
    depth = w_in.shape[0]
    alpha = (2 * depth) ** 0.25
    PW = _prep_weights(W)
    y_prompt = _encoder_trunk(x_prompt, meta, ln_emb_g, ln_emb_b, PW, alpha)
    y_sample = _encoder_trunk(x_sample, meta, ln_emb_g, ln_emb_b, PW, alpha)
    return (y_prompt, y_sample)
```
